```python
import jax, jax.numpy as jnp
from jax import lax
import numpy as np

D_MODEL = 2048
BATCH = 16
SEQ = 2048
DEPTH = 4

CHUNK = 64
N_EVEN = (DEPTH + 1) // 2
N_ODD = DEPTH // 2

SB_HEAD_DIM = 64
SB_WIDTH = D_MODEL // 2
SB_HEADS = SB_WIDTH // SB_HEAD_DIM
SB_BLOCK = 128

RW_HEAD_DIM = 64
RW_WIDTH = D_MODEL // 2
RW_HEADS = RW_WIDTH // RW_HEAD_DIM
DECAY_LORA = 64
ICL_LORA = 64
GATE_LORA = 160
LNX_EPS = 64e-5
RW_SHIFT_WIDTH = 3 * RW_WIDTH + DECAY_LORA + ICL_LORA + GATE_LORA

EVEN_IN_WIDTH = 3 * SB_WIDTH + RW_SHIFT_WIDTH
EVEN_OUT_WIDTH = SB_WIDTH + RW_WIDTH

SGU_WIDTH = D_MODEL
SGU_GROUPS = 8
SGU_BLOCK = 128
LN_EPS = 1e-5

FFN_HIDDEN = 4 * D_MODEL
RMS_EPS = 1e-6

kernel_name = "hybrid_stickbreak_rwkv7_sgu_stream_encoder"

F32 = jnp.float32


def rms_norm(x, g):
    x32 = x.astype(F32)
    y = x32 * lax.rsqrt(jnp.mean(x32 * x32, axis=-1, keepdims=True) + RMS_EPS)
    return (y * g.astype(F32)).astype(x.dtype)


def token_shift(z, mu):
    prev = jnp.pad(z[:, :-1], ((0, 0), (1, 0), (0, 0)))
    return z + (prev - z) * mu


def stick_breaking_attention(q, k, v):
    T, Dh = q.shape[2], q.shape[3]
    scale = Dh ** -0.5
    outs = []
    for blk in range(T // SB_BLOCK):
        q0 = blk * SB_BLOCK
        kv_len = q0 + SB_BLOCK
        qb = q[:, :, q0:kv_len].astype(F32)
        kb = k[:, :, :kv_len].astype(F32)
        vb = v[:, :, :kv_len].astype(F32)
        z = jnp.einsum('bhtd,bhsd->bhts', qb, kb) * scale
        t_pos = q0 + jnp.arange(SB_BLOCK)
        s_pos = jnp.arange(kv_len)
        visible = s_pos[None, :] < t_pos[:, None]
        log_keep = jnp.where(visible, jax.nn.log_sigmoid(-z), 0.0)
        tail = lax.cumsum(log_keep, axis=3, reverse=True)
        tail_excl = jnp.concatenate([tail[..., 1:], jnp.zeros_like(tail[..., :1])], axis=-1)
        w = jnp.where(visible, jnp.exp(jax.nn.log_sigmoid(z) + tail_excl), 0.0)
        outs.append(jnp.einsum('bhts,bhsd->bhtd', w, vb))
    return jnp.concatenate(outs, axis=2).astype(q.dtype)


def rwkv7_time_mix(zb, mu, w_up, w0, a_up, a0, g_up, k_k, k_a, r_k, lnx_w, lnx_b):
    B, T, _ = zb.shape
    zs = token_shift(zb, mu)
    idx = [RW_WIDTH, 2 * RW_WIDTH, 3 * RW_WIDTH, 3 * RW_WIDTH + DECAY_LORA,
           3 * RW_WIDTH + DECAY_LORA + ICL_LORA]
    r, k, v, xw, xa, xg = jnp.split(zs, idx, axis=-1)

    w_log = -jax.nn.softplus(-(w0 + jnp.tanh(xw) @ w_up).astype(F32)) - 0.5
    decay = jnp.exp(-jnp.exp(w_log))
    a = jax.nn.sigmoid((a0 + xa @ a_up).astype(F32))
    g = jax.nn.sigmoid(xg) @ g_up

    def heads(t):
        return t.reshape(B, T, RW_HEADS, RW_HEAD_DIM)

    r32 = r.astype(F32)
    k32 = k.astype(F32)
    v32 = heads(v.astype(F32))
    kk = heads(k32 * k_k.astype(F32))
    kk = kk / jnp.maximum(jnp.sqrt(jnp.sum(kk * kk, axis=-1, keepdims=True)), 1e-12)
    k_mod = heads(k32 * (1.0 + (a - 1.0) * k_a.astype(F32)))
    r_h = heads(r32)
    a_h = heads(a)
    w_h = heads(decay)

    def step(S, inp):
        r_t, w_t, k_t, v_t, kk_t, a_t = inp
        sa = jnp.einsum('bhvk,bhk->bhv', S, -kk_t)
        S = (S * w_t[:, :, None, :] + sa[..., None] * (kk_t * a_t)[:, :, None, :]
             + v_t[..., None] * k_t[:, :, None, :])
        y = jnp.einsum('bhvk,bhk->bhv', S, r_t)
        return S, y

    xs = tuple(jnp.moveaxis(t, 1, 0) for t in (r_h, w_h, k_mod, v32, kk, a_h))
    S0 = jnp.zeros((B, RW_HEADS, RW_HEAD_DIM, RW_HEAD_DIM), F32)
    _, y = lax.scan(step, S0, xs)
    y = jnp.moveaxis(y, 0, 1)

    mean = jnp.mean(y, axis=-1, keepdims=True)
    var = jnp.mean(jnp.square(y - mean), axis=-1, keepdims=True)
    y = ((y - mean) * lax.rsqrt(var + LNX_EPS)).reshape(B, T, RW_WIDTH)
    y = y * lnx_w.astype(F32) + lnx_b.astype(F32)
    bonus = jnp.sum(r_h * k_mod * r_k.astype(F32), axis=-1, keepdims=True) * v32
    y = (y + bonus.reshape(B, T, RW_WIDTH)) * g.astype(F32)
    return y.astype(zb.dtype)


def even_mixer(h, w_in, mu, w_up, w0, a_up, a0, g_up, k_k, k_a, r_k, lnx_w, lnx_b, w_out):
    B, T, _ = h.shape
    z = h @ w_in
    z_sb, z_rw = z[..., :3 * SB_WIDTH], z[..., 3 * SB_WIDTH:]
    q, k, v = jnp.split(z_sb, 3, axis=-1)

    def sb_heads(t):
        return t.reshape(B, T, SB_HEADS, SB_HEAD_DIM).transpose(0, 2, 1, 3)

    o_sb = stick_breaking_attention(sb_heads(q), sb_heads(k), sb_heads(v))
    o_sb = o_sb.transpose(0, 2, 1, 3).reshape(B, T, SB_WIDTH)
    o_rw = rwkv7_time_mix(z_rw, mu, w_up, w0, a_up, a0, g_up, k_k, k_a, r_k, lnx_w, lnx_b)
    return jnp.concatenate([o_sb.astype(h.dtype), o_rw.astype(h.dtype)], axis=-1) @ w_out


def odd_mixer(h, w_in, ln_g, ln_b, w_s, b_s, w_out):
    B, T, _ = h.shape
    zc = jax.nn.gelu(h @ w_in, approximate=False)
    u, v = jnp.split(zc, 2, axis=-1)
    v32 = v.astype(F32)
    mean = jnp.mean(v32, axis=-1, keepdims=True)
    var = jnp.mean(jnp.square(v32 - mean), axis=-1, keepdims=True)
    v32 = (v32 - mean) * lax.rsqrt(var + LN_EPS) * ln_g.astype(F32) + ln_b.astype(F32)
    nb = T // SGU_BLOCK
    v32 = v32.reshape(B, nb, SGU_BLOCK, SGU_GROUPS, SGU_WIDTH // SGU_GROUPS)
    chunk_id = jnp.arange(SGU_BLOCK) // CHUNK
    mask = chunk_id[None, :] <= chunk_id[:, None]
    ws = jnp.where(mask[None], w_s.astype(F32), 0.0)
    mixed = jnp.einsum('gts,bnsgc->bntgc', ws, v32) + b_s.astype(F32).T[None, None, :, :, None]
    y = u * mixed.reshape(B, T, SGU_WIDTH).astype(u.dtype)
    return y @ w_out


def squared_relu_mlp(h, w1, w2):
    return jnp.square(jax.nn.relu(h @ w1)) @ w2


def setup_inputs(seed: int = 0) -> dict:
    key = jax.random.key(seed)
    ks = jax.random.split(key, 32)
    n = jax.random.normal
    D = D_MODEL
    inp = {}
    inp["x"] = n(ks[0], (BATCH, SEQ, D), F32)
    inp["norm_mix"] = 1.0 + 0.02 * n(ks[1], (DEPTH, D), F32)
    inp["norm_ffn"] = 1.0 + 0.02 * n(ks[2], (DEPTH, D), F32)
    inp["norm_final"] = 1.0 + 0.02 * n(ks[3], (D,), F32)
    inp["even_w_in"] = n(ks[4], (N_EVEN, D, EVEN_IN_WIDTH), F32) * D ** -0.5
    inp["even_mu"] = jax.random.uniform(ks[5], (N_EVEN, RW_SHIFT_WIDTH), F32)
    inp["even_w_up"] = n(ks[6], (N_EVEN, DECAY_LORA, RW_WIDTH), F32) * 0.5 * DECAY_LORA ** -0.5
    inp["even_w0"] = 0.5 * n(ks[7], (N_EVEN, RW_WIDTH), F32)
    inp["even_a_up"] = n(ks[8], (N_EVEN, ICL_LORA, RW_WIDTH), F32) * ICL_LORA ** -0.5
    inp["even_a0"] = 0.1 * n(ks[9], (N_EVEN, RW_WIDTH), F32)
    inp["even_g_up"] = n(ks[10], (N_EVEN, GATE_LORA, RW_WIDTH), F32) * GATE_LORA ** -0.5
    inp["even_k_k"] = 0.85 + 0.02 * n(ks[11], (N_EVEN, RW_WIDTH), F32)
    inp["even_k_a"] = 1.0 + 0.02 * n(ks[12], (N_EVEN, RW_WIDTH), F32)
    inp["even_r_k"] = 0.1 * n(ks[13], (N_EVEN, RW_HEADS, RW_HEAD_DIM), F32)
    inp["even_lnx_w"] = 1.0 + 0.02 * n(ks[14], (N_EVEN, RW_WIDTH), F32)
    inp["even_lnx_b"] = 0.02 * n(ks[15], (N_EVEN, RW_WIDTH), F32)
    inp["even_w_out"] = n(ks[16], (N_EVEN, EVEN_OUT_WIDTH, D), F32) * EVEN_OUT_WIDTH ** -0.5
    inp["odd_w_in"] = n(ks[17], (N_ODD, D, 2 * SGU_WIDTH), F32) * D ** -0.5
    inp["odd_ln_g"] = 1.0 + 0.02 * n(ks[18], (N_ODD, SGU_WIDTH), F32)
    inp["odd_ln_b"] = 0.02 * n(ks[19], (N_ODD, SGU_WIDTH), F32)
    inp["odd_w_s"] = n(ks[20], (N_ODD, SGU_GROUPS, SGU_BLOCK, SGU_BLOCK), F32) * SGU_BLOCK ** -0.5
    inp["odd_b_s"] = 1.0 + 0.1 * n(ks[21], (N_ODD, SGU_GROUPS, SGU_BLOCK), F32)
    inp["odd_w_out"] = n(ks[22], (N_ODD, SGU_WIDTH, D), F32) * SGU_WIDTH ** -0.5
    inp["ffn_w1"] = n(ks[23], (DEPTH, D, FFN_HIDDEN), F32) * D ** -0.5
    inp["ffn_w2"] = n(ks[24], (DEPTH, FFN_HIDDEN, D), F32) * FFN_HIDDEN ** -0.5
    return inp


def reference(x, norm_mix, norm_ffn, norm_final,
              even_w_in, even_mu, even_w_up, even_w0, even_a_up, even_a0, even_g_up,
              even_k_k, even_k_a, even_r_k, even_lnx_w, even_lnx_b, even_w_out,
              odd_w_in, odd_ln_g, odd_ln_b, odd_w_s, odd_b_s, odd_w_out,
              ffn_w1, ffn_w2):
    for layer in range(DEPTH):
        i = layer // 2
        h = rms_norm(x, norm_mix[layer])
        if layer % 2 == 0:
            mix = even_mixer(h, even_w_in[i], even_mu[i], even_w_up[i], even_w0[i],
                             even_a_up[i], even_a0[i], even_g_up[i], even_k_k[i],
                             even_k_a[i], even_r_k[i], even_lnx_w[i], even_lnx_b[i],
                             even_w_out[i])
        else:
            mix = odd_mixer(h, odd_w_in[i], odd_ln_g[i], odd_ln_b[i], odd_w_s[i],
                            odd_b_s[i], odd_w_out[i])
        x = x + mix.astype(x.dtype)
        h = rms_norm(x, norm_ffn[layer])
        x = x + squared_relu_mlp(h, ffn_w1[layer], ffn_w2[layer]).astype(x.dtype)
    return rms_norm(x, norm_final)
```

```python
import functools
import math

import jax
import jax.numpy as jnp
from jax import lax
from jax.experimental import pallas as pl
from jax.experimental.pallas import tpu as pltpu

F32 = jnp.float32
BF16 = jnp.bfloat16

LANES = 128
HEAD_DIM = 64
RMS_EPS = 1e-6
LN_EPS = 1e-5
LNX_EPS = 64e-5
SB_WIDTH = 1024
RW_WIDTH = 1024
DECAY_LORA = 64
ICL_LORA = 64
GATE_LORA = 160
LORA_PAD = 384
SGU_GROUPS = 8
SGU_BLOCK = 128
STREAM_CHUNK = 64
RW_CHUNK = 64
VMEM_LIMIT = 56 * 1024 * 1024


def _params(*sem):
    return pltpu.CompilerParams(dimension_semantics=sem, vmem_limit_bytes=VMEM_LIMIT)


def _rms(x, g):
    return x * lax.rsqrt(jnp.mean(x * x, axis=-1, keepdims=True) + RMS_EPS) * g


def _dot(a, b):
    return jnp.dot(a, b, preferred_element_type=F32)


def _dot_nt(a, b):
    return lax.dot_general(a, b, (((1,), (1,)), ((), ())), preferred_element_type=F32)


def _dot_tn(a, b):
    return lax.dot_general(a, b, (((0,), (0,)), ((), ())), preferred_element_type=F32)


def _split_bf16(x):
    hi = x.astype(BF16)
    lo = (x - hi.astype(F32)).astype(BF16)
    return hi, lo


def _dot_hilo(x, w2):
    hi, lo = _split_bf16(x)
    return _dot(jnp.concatenate([hi, lo], axis=1), w2)


def _norm_matmul_kernel(x_ref, g_ref, w_ref, o_ref, h_ref, *, gelu):
    @pl.when(pl.program_id(1) == 0)
    def _():
        h_ref[...] = _rms(x_ref[...], g_ref[...]).astype(BF16)

    y = _dot(h_ref[...], w_ref[...])
    if gelu:
        y = 0.5 * y * (1.0 + lax.erf(y * math.sqrt(0.5)))
    o_ref[...] = y.astype(o_ref.dtype)


def _norm_matmul(x, g, w, *, tm, tn, gelu=False, out_dtype=F32):
    n, d = x.shape
    nout = w.shape[1]
    return pl.pallas_call(
        functools.partial(_norm_matmul_kernel, gelu=gelu),
        grid=(n // tm, nout // tn),
        in_specs=[
            pl.BlockSpec((tm, d), lambda i, j: (i, 0)),
            pl.BlockSpec((1, d), lambda i, j: (0, 0)),
            pl.BlockSpec((d, tn), lambda i, j: (0, j)),
        ],
        out_specs=pl.BlockSpec((tm, tn), lambda i, j: (i, j)),
        out_shape=jax.ShapeDtypeStruct((n, nout), out_dtype),
        scratch_shapes=[pltpu.VMEM((tm, d), BF16)],
        compiler_params=_params("parallel", "arbitrary"),
        name="norm_matmul_gelu" if gelu else "norm_matmul",
    )(x, g, w)


def _ffn_kernel(x_ref, g_ref, w1_ref, w2_ref, gf_ref, o_ref, h_ref, *, final_norm):
    j = pl.program_id(1)

    @pl.when(j == 0)
    def _():
        x = x_ref[...]
        h_ref[...] = _rms(x, g_ref[...]).astype(BF16)
        o_ref[...] = x

    a = jnp.square(jnp.maximum(_dot(h_ref[...], w1_ref[...]), 0.0)).astype(BF16)
    o_ref[...] += _dot(a, w2_ref[...])

    if final_norm:
        @pl.when(j == pl.num_programs(1) - 1)
        def _():
            o_ref[...] = _rms(o_ref[...], gf_ref[...])


def _ffn(x, g, w1, w2, gf, *, tm, th, final_norm):
    n, d = x.shape
    hid = w1.shape[1]
    return pl.pallas_call(
        functools.partial(_ffn_kernel, final_norm=final_norm),
        grid=(n // tm, hid // th),
        in_specs=[
            pl.BlockSpec((tm, d), lambda i, j: (i, 0)),
            pl.BlockSpec((1, d), lambda i, j: (0, 0)),
            pl.BlockSpec((d, th), lambda i, j: (0, j)),
            pl.BlockSpec((th, d), lambda i, j: (j, 0)),
            pl.BlockSpec((1, d), lambda i, j: (0, 0)),
        ],
        out_specs=pl.BlockSpec((tm, d), lambda i, j: (i, 0)),
        out_shape=jax.ShapeDtypeStruct((n, d), F32),
        scratch_shapes=[pltpu.VMEM((tm, d), BF16)],
        compiler_params=_params("parallel", "arbitrary"),
        name="ffn_final" if final_norm else "ffn",
    )(x, g, w1, w2, gf)


def _proj2_kernel(x_ref, a_ref, b_ref, w_ref, o_ref):
    ka = a_ref.shape[1]
    o_ref[...] = (x_ref[...] + _dot(a_ref[...], w_ref[:ka, :]) + _dot(b_ref[...], w_ref[ka:, :]))


def _proj2(x, a, b, w, *, tm):
    n, d = x.shape
    ka, kb = a.shape[1], b.shape[1]
    return pl.pallas_call(
        _proj2_kernel,
        grid=(n // tm,),
        in_specs=[
            pl.BlockSpec((tm, d), lambda i: (i, 0)),
            pl.BlockSpec((tm, ka), lambda i: (i, 0)),
            pl.BlockSpec((tm, kb), lambda i: (i, 0)),
            pl.BlockSpec((ka + kb, d), lambda i: (0, 0)),
        ],
        out_specs=pl.BlockSpec((tm, d), lambda i: (i, 0)),
        out_shape=jax.ShapeDtypeStruct((n, d), F32),
        compiler_params=_params("parallel"),
        name="out_proj",
    )(x, a, b, w)


def _sb_kernel(q_ref, k_ref, v_ref, o_ref, *, tq):
    qi = pl.program_id(2)
    lane = lax.broadcasted_iota(jnp.int32, (tq, LANES), 1)
    head0 = lane < HEAD_DIM
    q = q_ref[...] * (HEAD_DIM ** -0.5)
    qh = (jnp.where(head0, q, 0.0).astype(BF16), jnp.where(head0, 0.0, q).astype(BF16))

    jj = lax.broadcasted_iota(jnp.int32, (2 * tq, 2 * tq), 0)
    ss = lax.broadcasted_iota(jnp.int32, (2 * tq, 2 * tq), 1)
    tail_mat = jnp.where((ss >= tq) | ((jj & (tq - 1)) > ss), 1.0, 0.0).astype(BF16)

    tt = lax.broadcasted_iota(jnp.int32, (tq, tq), 0)
    sk = lax.broadcasted_iota(jnp.int32, (tq, tq), 1)
    visible = sk < tt

    def tile(j, carry, diag):
        rows = pl.ds(pl.multiple_of(j * tq, tq), tq)
        kb = k_ref[rows, :].astype(BF16)
        vb = v_ref[rows, :].astype(BF16)
        out = []
        for h in range(2):
            c, o = carry[h]
            z = _dot_nt(qh[h], kb)
            sp = jnp.maximum(z, 0.0) + jnp.log1p(jnp.exp(-jnp.abs(z)))
            log_keep = -sp
            if diag:
                log_keep = jnp.where(visible, log_keep, 0.0)
            both = _dot_hilo(log_keep, tail_mat)
            w = jnp.exp((z - sp) + both[:, :tq] + c)
            if diag:
                w = jnp.where(visible, w, 0.0)
            out.append((c + both[:, tq:], o + _dot(w.astype(BF16), vb)))
        return tuple(out)

    zero = jnp.zeros((tq, tq), F32)
    carry = tile(qi, ((zero, zero), (zero, zero)), True)
    carry = lax.fori_loop(0, qi, lambda i, c: tile(qi - 1 - i, c, False), carry)
    o_ref[...] = jnp.where(head0, carry[0][1], carry[1][1]).astype(o_ref.dtype)


def _sb_attention(z, *, batch, seq, tq):
    n = z.shape[0]
    nq = seq // tq
    pairs = SB_WIDTH // LANES
    return pl.pallas_call(
        functools.partial(_sb_kernel, tq=tq),
        grid=(batch, pairs, nq),
        in_specs=[
            pl.BlockSpec((tq, LANES), lambda b, p, i: (b * nq + i, p)),
            pl.BlockSpec((seq, LANES), lambda b, p, i: (b, pairs + p)),
            pl.BlockSpec((seq, LANES), lambda b, p, i: (b, 2 * pairs + p)),
        ],
        out_specs=pl.BlockSpec((tq, LANES), lambda b, p, i: (b * nq + i, p)),
        out_shape=jax.ShapeDtypeStruct((n, SB_WIDTH), BF16),
        compiler_params=_params("parallel", "parallel", "arbitrary"),
        name="stick_breaking",
    )(z, z, z)


def _rwkv_kernel(r_ref, k_ref, v_ref, lo_ref, mur_ref, muk_ref, muv_ref, mulo_ref,
                 wup_ref, w0_ref, aup_ref, a0_ref, gup_ref, kk_ref, ka_ref, rk_ref,
                 lnw_ref, lnb_ref, o_ref, s_ref, pr_ref, pk_ref, pv_ref, plo_ref, *, tc):
    C = RW_CHUNK
    C2 = 2 * C
    t = pl.program_id(2)

    @pl.when(t == 0)
    def _():
        s_ref[...] = jnp.zeros_like(s_ref)
        pr_ref[...] = jnp.zeros_like(pr_ref)
        pk_ref[...] = jnp.zeros_like(pk_ref)
        pv_ref[...] = jnp.zeros_like(pv_ref)
        plo_ref[...] = jnp.zeros_like(plo_ref)

    def shifted(z_ref, prev_ref, mu_ref):
        z = z_ref[...]
        row = lax.broadcasted_iota(jnp.int32, z.shape, 0)
        prev = jnp.where(row == 0, prev_ref[0:1, :], pltpu.roll(z, 1, axis=0))
        prev_ref[0:1, :] = z[tc - 1:tc, :]
        return z + (prev - z) * mu_ref[...]

    r = shifted(r_ref, pr_ref, mur_ref)
    k = shifted(k_ref, pk_ref, muk_ref)
    v = shifted(v_ref, pv_ref, muv_ref)
    lo = shifted(lo_ref, plo_ref, mulo_ref)
    xw = lo[:, :DECAY_LORA]
    xa = lo[:, DECAY_LORA:DECAY_LORA + ICL_LORA]
    xg = lo[:, DECAY_LORA + ICL_LORA:]

    wl = w0_ref[...] + _dot(jnp.tanh(xw).astype(BF16), wup_ref[...])
    w_log = -(jnp.maximum(-wl, 0.0) + jnp.log1p(jnp.exp(-jnp.abs(wl)))) - 0.5
    lw = -jnp.exp(w_log)
    a = jax.nn.sigmoid(a0_ref[...] + _dot(xa.astype(BF16), aup_ref[...]))
    g = _dot(jax.nn.sigmoid(xg).astype(BF16), gup_ref[...])

    li = lax.broadcasted_iota(jnp.int32, (2 * LANES, LANES), 0)
    lj = lax.broadcasted_iota(jnp.int32, (2 * LANES, LANES), 1)
    head_sum = jnp.where((li & (LANES - 1)) // HEAD_DIM == lj // HEAD_DIM, 1.0, 0.0).astype(BF16)

    kk = k * kk_ref[...]
    kk = kk / jnp.maximum(jnp.sqrt(_dot_hilo(kk * kk, head_sum)), 1e-12)
    kmod = k * (1.0 + (a - 1.0) * ka_ref[...])
    bonus = _dot_hilo(r * kmod * rk_ref[...], head_sum) * v
    kka = kk * a

    lane = lax.broadcasted_iota(jnp.int32, (C, LANES), 1)
    head0 = lane < HEAD_DIM

    def stack(x):
        return jnp.concatenate([jnp.where(head0, x, 0.0), jnp.where(head0, 0.0, x)], axis=0)

    ci = lax.broadcasted_iota(jnp.int32, (C, C2), 0)
    cj = lax.broadcasted_iota(jnp.int32, (C, C2), 1)
    cum_mat = jnp.where((cj & (C - 1)) <= ci, 1.0, 0.0).astype(BF16)

    gi = lax.broadcasted_iota(jnp.int32, (2 * C2, 2 * C2), 0)
    gj = lax.broadcasted_iota(jnp.int32, (2 * C2, 2 * C2), 1)
    ti, tj = gi & (C - 1), gj & (C - 1)
    gram_mask = tj < ti + jnp.where(gi < C2, 0, 1)
    si = lax.broadcasted_iota(jnp.int32, (C2, C2), 0)
    sj = lax.broadcasted_iota(jnp.int32, (C2, C2), 1)
    eye = si == sj
    s_mask = (si // C) == (sj // C)

    for c in range(tc // C):
        rows = slice(c * C, (c + 1) * C)
        lw_c = lw[rows]
        cum = _dot(cum_mat, jnp.concatenate(_split_bf16(lw_c), axis=0))
        e_cum = jnp.exp(cum)
        e_inv = jnp.exp(-cum)
        rt = r[rows] * e_cum
        bt = kk[rows] * jnp.exp(cum - lw_c)
        at = -kka[rows] * e_inv
        kt = kmod[rows] * e_inv
        v_c = v[rows]

        lhs = jnp.concatenate([stack(bt), stack(rt)], axis=0).astype(BF16)
        rhs = jnp.concatenate([stack(at), stack(kt)], axis=0).astype(BF16)
        gram = jnp.where(gram_mask, _dot_nt(lhs, rhs), 0.0)
        n_ab = gram[:C2, :C2]
        a_ra = gram[C2:, :C2]
        m_v = gram[:, C2:].astype(BF16)

        tinv = jnp.where(eye, 1.0, n_ab)
        p = n_ab
        for _ in range(int(math.log2(C)) - 1):
            pb = p.astype(BF16)
            p = _dot(pb, pb)
            tinv = tinv + _dot(tinv.astype(BF16), p.astype(BF16))

        s0 = s_ref[...]
        hv = _dot_nt(jnp.concatenate([bt, rt], axis=0).astype(BF16), s0.astype(BF16))
        mv = _dot(m_v, stack(v_c).astype(BF16))
        x = hv[:C] + mv[:C] + mv[C:C2]
        ux = _dot(tinv.astype(BF16), stack(x).astype(BF16))
        u = ux[:C] + ux[C:]
        au = _dot(a_ra.astype(BF16), stack(u).astype(BF16))
        y = hv[C:] + mv[C2:C2 + C] + mv[C2 + C:] + au[:C] + au[C:]

        ds = _dot_tn(jnp.concatenate([u, v_c], axis=0).astype(BF16),
                     jnp.concatenate([at, kt], axis=0).astype(BF16))
        s_ref[...] = (s0 + jnp.where(s_mask, ds, 0.0)) * e_cum[C - 1:C, :]

        mean = _dot_hilo(y, head_sum) * (1.0 / HEAD_DIM)
        yc = y - mean
        var = _dot_hilo(yc * yc, head_sum) * (1.0 / HEAD_DIM)
        yn = yc * lax.rsqrt(var + LNX_EPS) * lnw_ref[...] + lnb_ref[...]
        o_ref[rows, :] = ((yn + bonus[rows]) * g[rows]).astype(o_ref.dtype)


def _rwkv(z, mu, w_up, w0, a_up, a0, g_up, k_k, k_a, r_k, lnx_w, lnx_b, *, batch, seq, tc):
    n = z.shape[0]
    nt = seq // tc
    pairs = RW_WIDTH // LANES
    col0 = 3 * SB_WIDTH // LANES
    lo_blk = (3 * SB_WIDTH + 3 * RW_WIDTH) // LORA_PAD
    row = lambda b, p, t: (b * nt + t, 0)
    vec = lambda off: pl.BlockSpec((1, LANES), lambda b, p, t: (0, off + p))
    return pl.pallas_call(
        functools.partial(_rwkv_kernel, tc=tc),
        grid=(batch, pairs, nt),
        in_specs=[
            pl.BlockSpec((tc, LANES), lambda b, p, t: (b * nt + t, col0 + p)),
            pl.BlockSpec((tc, LANES), lambda b, p, t: (b * nt + t, col0 + pairs + p)),
            pl.BlockSpec((tc, LANES), lambda b, p, t: (b * nt + t, col0 + 2 * pairs + p)),
            pl.BlockSpec((tc, LORA_PAD), lambda b, p, t: (b * nt + t, lo_blk)),
            vec(0), vec(pairs), vec(2 * pairs),
            pl.BlockSpec((1, LORA_PAD), lambda b, p, t: (0, 3 * RW_WIDTH // LORA_PAD)),
            pl.BlockSpec((DECAY_LORA, LANES), lambda b, p, t: (0, p)),
            vec(0),
            pl.BlockSpec((ICL_LORA, LANES), lambda b, p, t: (0, p)),
            vec(0),
            pl.BlockSpec((LORA_PAD - DECAY_LORA - ICL_LORA, LANES), lambda b, p, t: (0, p)),
            vec(0), vec(0), vec(0), vec(0), vec(0),
        ],
        out_specs=pl.BlockSpec((tc, LANES), lambda b, p, t: (b * nt + t, p)),
        out_shape=jax.ShapeDtypeStruct((n, RW_WIDTH), BF16),
        scratch_shapes=[
            pltpu.VMEM((LANES, LANES), F32),
            pltpu.VMEM((8, LANES), F32), pltpu.VMEM((8, LANES), F32), pltpu.VMEM((8, LANES), F32),
            pltpu.VMEM((8, LORA_PAD), F32),
        ],
        compiler_params=_params("parallel", "parallel", "arbitrary"),
        name="rwkv7",
    )(z, z, z, z, mu, mu, mu, mu, w_up, w0, a_up, a0, g_up, k_k, k_a, r_k, lnx_w, lnx_b)


def _sgu_kernel(x_ref, u_ref, v_ref, lg_ref, lb_ref, ws_ref, bs_ref, wo_ref, o_ref, y_ref, *, tm):
    v = v_ref[...]
    mean = jnp.mean(v, axis=-1, keepdims=True)
    vc = v - mean
    var = jnp.mean(vc * vc, axis=-1, keepdims=True)
    vn = (vc * lax.rsqrt(var + LN_EPS) * lg_ref[...] + lb_ref[...]).astype(BF16)

    ti = lax.broadcasted_iota(jnp.int32, (SGU_BLOCK, SGU_BLOCK), 0) // STREAM_CHUNK
    si = lax.broadcasted_iota(jnp.int32, (SGU_BLOCK, SGU_BLOCK), 1) // STREAM_CHUNK
    causal = si <= ti
    gw = v.shape[1] // SGU_GROUPS
    for g in range(SGU_GROUPS):
        ws = jnp.where(causal, ws_ref[g], 0.0).astype(BF16)
        cols = slice(g * gw, (g + 1) * gw)
        for blk in range(tm // SGU_BLOCK):
            rows = slice(blk * SGU_BLOCK, (blk + 1) * SGU_BLOCK)
            mixed = _dot(ws, vn[rows, cols]) + bs_ref[:, g:g + 1]
            y_ref[rows, cols] = (u_ref[rows, cols] * mixed).astype(BF16)
    o_ref[...] = x_ref[...] + _dot(y_ref[...], wo_ref[...])


def _sgu(x, zc, ln_g, ln_b, w_s, b_s_t, w_out, *, tm):
    n, d = x.shape
    return pl.pallas_call(
        functools.partial(_sgu_kernel, tm=tm),
        grid=(n // tm,),
        in_specs=[
            pl.BlockSpec((tm, d), lambda i: (i, 0)),
            pl.BlockSpec((tm, d), lambda i: (i, 0)),
            pl.BlockSpec((tm, d), lambda i: (i, 1)),
            pl.BlockSpec((1, d), lambda i: (0, 0)),
            pl.BlockSpec((1, d), lambda i: (0, 0)),
            pl.BlockSpec((SGU_GROUPS, SGU_BLOCK, SGU_BLOCK), lambda i: (0, 0, 0)),
            pl.BlockSpec((SGU_BLOCK, SGU_GROUPS), lambda i: (0, 0)),
            pl.BlockSpec((d, d), lambda i: (0, 0)),
        ],
        out_specs=pl.BlockSpec((tm, d), lambda i: (i, 0)),
        out_shape=jax.ShapeDtypeStruct((n, d), F32),
        scratch_shapes=[pltpu.VMEM((tm, d), BF16)],
        compiler_params=_params("parallel"),
        name="sgu",
    )(x, zc, zc, ln_g, ln_b, w_s, b_s_t, w_out)


def _pad_cols(w, width):
    return jnp.pad(w, ((0, 0), (0, width - w.shape[1])))


def _pad_rows(w, height):
    return jnp.pad(w, ((0, height - w.shape[0]), (0, 0)))


def kernel(x, norm_mix, norm_ffn, norm_final, even_w_in, even_mu, even_w_up, even_w0, even_a_up, even_a0, even_g_up, even_k_k, even_k_a, even_r_k, even_lnx_w, even_lnx_b, even_w_out, odd_w_in, odd_ln_g, odd_ln_b, odd_w_s, odd_b_s, odd_w_out, ffn_w1, ffn_w2):
    batch, seq, d = x.shape
    depth = norm_mix.shape[0]
    xf = x.reshape(batch * seq, d)
    row = lambda a: a.reshape(1, -1)

    rw_cols = 3 * RW_WIDTH + LORA_PAD
    in_cols = 3 * SB_WIDTH + rw_cols

    for layer in range(depth):
        i = layer // 2
        if layer % 2 == 0:
            w_in = _pad_cols(even_w_in[i], in_cols).astype(BF16)
            z = _norm_matmul(xf, row(norm_mix[layer]), w_in, tm=512, tn=in_cols // 3)
            o_sb = _sb_attention(z, batch=batch, seq=seq, tq=128)
            o_rw = _rwkv(z, _pad_cols(row(even_mu[i]), rw_cols),
                         even_w_up[i].astype(BF16), row(even_w0[i]),
                         even_a_up[i].astype(BF16), row(even_a0[i]),
                         _pad_rows(even_g_up[i], LORA_PAD - DECAY_LORA - ICL_LORA).astype(BF16),
                         row(even_k_k[i]), row(even_k_a[i]),
                         row(even_r_k[i]), row(even_lnx_w[i]), row(even_lnx_b[i]),
                         batch=batch, seq=seq, tc=256)
            xf = _proj2(xf, o_sb, o_rw, even_w_out[i].astype(BF16), tm=512)
        else:
            zc = _norm_matmul(xf, row(norm_mix[layer]), odd_w_in[i].astype(BF16),
                              tm=512, tn=1024, gelu=True)
            xf = _sgu(xf, zc, row(odd_ln_g[i]), row(odd_ln_b[i]), odd_w_s[i],
                      odd_b_s[i].T, odd_w_out[i].astype(BF16), tm=256)
        xf = _ffn(xf, row(norm_ffn[layer]), ffn_w1[layer].astype(BF16), ffn_w2[layer].astype(BF16),
                  row(norm_final), tm=512, th=1024, final_norm=(layer == depth - 1))
    return xf.reshape(batch, seq, d)
```

```python
import functools
import math

import jax
import jax.numpy as jnp
from jax import lax
from jax.experimental import pallas as pl
from jax.experimental.pallas import tpu as pltpu

F32 = jnp.float32
BF16 = jnp.bfloat16

LANES = 128
HEAD_DIM = 64
RMS_EPS = 1e-6
LN_EPS = 1e-5
LNX_EPS = 64e-5
SB_WIDTH = 1024
RW_WIDTH = 1024
DECAY_LORA = 64
ICL_LORA = 64
GATE_LORA = 160
LORA_PAD = 384
SGU_GROUPS = 8
SGU_BLOCK = 128
STREAM_CHUNK = 64
RW_CHUNK = 64
VMEM_LIMIT = 56 * 1024 * 1024


def _params(*sem):
    return pltpu.CompilerParams(dimension_semantics=sem, vmem_limit_bytes=VMEM_LIMIT)


def _rms(x, g):
    return x * lax.rsqrt(jnp.mean(x * x, axis=-1, keepdims=True) + RMS_EPS) * g


def _dot(a, b):
    return jnp.dot(a, b, preferred_element_type=F32)


def _dot_nt(a, b):
    return lax.dot_general(a, b, (((1,), (1,)), ((), ())), preferred_element_type=F32)


def _dot_tn(a, b):
    return lax.dot_general(a, b, (((0,), (0,)), ((), ())), preferred_element_type=F32)


def _split_bf16(x):
    hi = x.astype(BF16)
    lo = (x - hi.astype(F32)).astype(BF16)
    return hi, lo


def _dot_hilo(x, w2):
    hi, lo = _split_bf16(x)
    return _dot(jnp.concatenate([hi, lo], axis=1), w2)


def _norm_matmul_kernel(x_ref, g_ref, w_ref, o_ref, h_ref, *, gelu):
    @pl.when(pl.program_id(1) == 0)
    def _():
        h_ref[...] = _rms(x_ref[...], g_ref[...]).astype(BF16)

    y = _dot(h_ref[...], w_ref[...])
    if gelu:
        y = 0.5 * y * (1.0 + lax.erf(y * math.sqrt(0.5)))
    o_ref[...] = y.astype(o_ref.dtype)


def _norm_matmul(x, g, w, *, tm, tn, gelu=False, out_dtype=F32):
    n, d = x.shape
    nout = w.shape[1]
    return pl.pallas_call(
        functools.partial(_norm_matmul_kernel, gelu=gelu),
        grid=(n // tm, nout // tn),
        in_specs=[
            pl.BlockSpec((tm, d), lambda i, j: (i, 0)),
            pl.BlockSpec((1, d), lambda i, j: (0, 0)),
            pl.BlockSpec((d, tn), lambda i, j: (0, j)),
        ],
        out_specs=pl.BlockSpec((tm, tn), lambda i, j: (i, j)),
        out_shape=jax.ShapeDtypeStruct((n, nout), out_dtype),
        scratch_shapes=[pltpu.VMEM((tm, d), BF16)],
        compiler_params=_params("parallel", "arbitrary"),
        name="norm_matmul_gelu" if gelu else "norm_matmul",
    )(x, g, w)


def _ffn_kernel(x_ref, g_ref, w1_ref, w2_ref, gf_ref, o_ref, h_ref, *, final_norm):
    j = pl.program_id(1)

    @pl.when(j == 0)
    def _():
        x = x_ref[...]
        h_ref[...] = _rms(x, g_ref[...]).astype(BF16)
        o_ref[...] = x

    a = jnp.square(jnp.maximum(_dot(h_ref[...], w1_ref[...]), 0.0)).astype(BF16)
    o_ref[...] += _dot(a, w2_ref[...])

    if final_norm:
        @pl.when(j == pl.num_programs(1) - 1)
        def _():
            o_ref[...] = _rms(o_ref[...], gf_ref[...])


def _ffn(x, g, w1, w2, gf, *, tm, th, final_norm):
    n, d = x.shape
    hid = w1.shape[1]
    return pl.pallas_call(
        functools.partial(_ffn_kernel, final_norm=final_norm),
        grid=(n // tm, hid // th),
        in_specs=[
            pl.BlockSpec((tm, d), lambda i, j: (i, 0)),
            pl.BlockSpec((1, d), lambda i, j: (0, 0)),
            pl.BlockSpec((d, th), lambda i, j: (0, j)),
            pl.BlockSpec((th, d), lambda i, j: (j, 0)),
            pl.BlockSpec((1, d), lambda i, j: (0, 0)),
        ],
        out_specs=pl.BlockSpec((tm, d), lambda i, j: (i, 0)),
        out_shape=jax.ShapeDtypeStruct((n, d), F32),
        scratch_shapes=[pltpu.VMEM((tm, d), BF16)],
        compiler_params=_params("parallel", "arbitrary"),
        name="ffn_final" if final_norm else "ffn",
    )(x, g, w1, w2, gf)


def _proj2_kernel(x_ref, a_ref, b_ref, w_ref, o_ref):
    ka = a_ref.shape[1]
    o_ref[...] = (x_ref[...] + _dot(a_ref[...], w_ref[:ka, :]) + _dot(b_ref[...], w_ref[ka:, :]))


def _proj2(x, a, b, w, *, tm):
    n, d = x.shape
    ka, kb = a.shape[1], b.shape[1]
    return pl.pallas_call(
        _proj2_kernel,
        grid=(n // tm,),
        in_specs=[
            pl.BlockSpec((tm, d), lambda i: (i, 0)),
            pl.BlockSpec((tm, ka), lambda i: (i, 0)),
            pl.BlockSpec((tm, kb), lambda i: (i, 0)),
            pl.BlockSpec((ka + kb, d), lambda i: (0, 0)),
        ],
        out_specs=pl.BlockSpec((tm, d), lambda i: (i, 0)),
        out_shape=jax.ShapeDtypeStruct((n, d), F32),
        compiler_params=_params("parallel"),
        name="out_proj",
    )(x, a, b, w)


NEG_BIG = -1e30
SB_KEY_TILE = 128


def _sb_kernel(q_ref, k_ref, v_ref, o_ref, z_ref, hilo_ref, ls_ref, e_ref, rs_ref, c_ref, acc_ref, *, tq):
    tk = SB_KEY_TILE
    per_q = tq // tk
    assert per_q <= 2
    qi = pl.program_id(2)
    n_tiles = (qi + 1) * per_q
    lane = lax.broadcasted_iota(jnp.int32, (tq, LANES), 1)
    head0 = lane < HEAD_DIM
    q = q_ref[...] * (HEAD_DIM ** -0.5)
    qh = (jnp.where(head0, q, 0.0).astype(BF16), jnp.where(head0, 0.0, q).astype(BF16))

    jj = lax.broadcasted_iota(jnp.int32, (2 * tk, 2 * tk), 0)
    ss = lax.broadcasted_iota(jnp.int32, (2 * tk, 2 * tk), 1)
    tail_mat = jnp.where((ss >= tk) | ((jj & (tk - 1)) > ss), 1.0, 0.0).astype(BF16)

    def key_tile(m):
        return jnp.maximum(n_tiles - 1 - m, 0)

    def tile_rows(m):
        return pl.ds(pl.multiple_of(key_tile(m) * tk, tk), tk)

    def stage_a1(m, slot):
        kb = k_ref[tile_rows(m), :].astype(BF16)
        for h in range(2):
            z_ref[h, slot] = _dot_nt(qh[h], kb)

    def stage_a2(slot, masked_tile=None):
        for h in range(2):
            z = z_ref[h, slot]
            sp = jnp.maximum(z, 0.0) + jnp.log(1.0 + jnp.exp(-jnp.abs(z)))
            log_keep = -sp
            log_beta = z - sp
            if masked_tile is not None:
                t_pos = qi * tq + lax.broadcasted_iota(jnp.int32, (tq, tk), 0)
                s_pos = key_tile(masked_tile) * tk + lax.broadcasted_iota(jnp.int32, (tq, tk), 1)
                log_keep = jnp.where(s_pos < t_pos, log_keep, 0.0)
                log_beta = jnp.where(s_pos < t_pos, log_beta, NEG_BIG)
            hi, lo = _split_bf16(log_keep)
            hilo_ref[h, slot, :, :tk] = hi
            hilo_ref[h, slot, :, tk:] = lo
            ls_ref[h, slot] = log_beta

    def stage_b(slot):
        for h in range(2):
            both = _dot(hilo_ref[h, slot], tail_mat)
            e_ref[h, slot] = ls_ref[h, slot] + both[:, :tk]
            rs_ref[h, slot] = both[:, tk:]

    def stage_c(m, slot):
        vb = v_ref[tile_rows(m), :].astype(BF16)
        for h in range(2):
            c = c_ref[h]
            w = jnp.exp(e_ref[h, slot] + c)
            acc_ref[h] += _dot(w.astype(BF16), vb)
            c_ref[h] = c + rs_ref[h, slot]

    c_ref[...] = jnp.zeros_like(c_ref)
    acc_ref[...] = jnp.zeros_like(acc_ref)

    stage_a1(0, 0)
    stage_a2(0, masked_tile=0)
    stage_a1(1, 1)
    stage_b(0)
    stage_a2(1, masked_tile=1 if per_q > 1 else None)
    stage_a1(2, 0)

    @pl.loop(3, n_tiles + 3)
    def _(i):
        stage_c(i - 3, (i + 1) & 1)
        stage_b(i & 1)
        stage_a2((i + 1) & 1)
        stage_a1(i, i & 1)

    o_ref[...] = jnp.where(head0, acc_ref[0], acc_ref[1]).astype(o_ref.dtype)


def _sb_attention(z, *, batch, seq, tq):
    n = z.shape[0]
    nq = seq // tq
    pairs = SB_WIDTH // LANES
    return pl.pallas_call(
        functools.partial(_sb_kernel, tq=tq),
        grid=(batch, pairs, nq),
        in_specs=[
            pl.BlockSpec((tq, LANES), lambda b, p, i: (b * nq + i, p)),
            pl.BlockSpec((seq, LANES), lambda b, p, i: (b, pairs + p)),
            pl.BlockSpec((seq, LANES), lambda b, p, i: (b, 2 * pairs + p)),
        ],
        out_specs=pl.BlockSpec((tq, LANES), lambda b, p, i: (b * nq + i, p)),
        out_shape=jax.ShapeDtypeStruct((n, SB_WIDTH), BF16),
        scratch_shapes=[
            pltpu.VMEM((2, 2, tq, SB_KEY_TILE), F32),
            pltpu.VMEM((2, 2, tq, 2 * SB_KEY_TILE), BF16),
            pltpu.VMEM((2, 2, tq, SB_KEY_TILE), F32),
            pltpu.VMEM((2, 2, tq, SB_KEY_TILE), F32),
            pltpu.VMEM((2, 2, tq, SB_KEY_TILE), F32),
            pltpu.VMEM((2, tq, SB_KEY_TILE), F32),
            pltpu.VMEM((2, tq, LANES), F32),
        ],
        compiler_params=_params("parallel", "parallel", "arbitrary"),
        name="stick_breaking",
    )(z, z, z)


def _rwkv_kernel(r_ref, k_ref, v_ref, lo_ref, mur_ref, muk_ref, muv_ref, mulo_ref,
                 wup_ref, w0_ref, aup_ref, a0_ref, gup_ref, kk_ref, ka_ref, rk_ref,
                 lnw_ref, lnb_ref, o_ref, s_ref, pr_ref, pk_ref, pv_ref, plo_ref, *, tc):
    C = RW_CHUNK
    C2 = 2 * C
    nchunk = tc // C
    t = pl.program_id(2)

    @pl.when(t == 0)
    def _():
        s_ref[...] = jnp.zeros_like(s_ref)
        pr_ref[...] = jnp.zeros_like(pr_ref)
        pk_ref[...] = jnp.zeros_like(pk_ref)
        pv_ref[...] = jnp.zeros_like(pv_ref)
        plo_ref[...] = jnp.zeros_like(plo_ref)

    def shifted(z_ref, prev_ref, mu_ref):
        z = z_ref[...]
        row = lax.broadcasted_iota(jnp.int32, z.shape, 0)
        prev = jnp.where(row == 0, prev_ref[0:1, :], pltpu.roll(z, 1, axis=0))
        prev_ref[0:1, :] = z[tc - 1:tc, :]
        return z + (prev - z) * mu_ref[...]

    r = shifted(r_ref, pr_ref, mur_ref)
    k = shifted(k_ref, pk_ref, muk_ref)
    v = shifted(v_ref, pv_ref, muv_ref)
    lo = shifted(lo_ref, plo_ref, mulo_ref)
    xw = lo[:, :DECAY_LORA]
    xa = lo[:, DECAY_LORA:DECAY_LORA + ICL_LORA]
    xg = lo[:, DECAY_LORA + ICL_LORA:]

    wl = w0_ref[...] + _dot(jnp.tanh(xw).astype(BF16), wup_ref[...])
    w_log = -(jnp.maximum(-wl, 0.0) + jnp.log1p(jnp.exp(-jnp.abs(wl)))) - 0.5
    lw = -jnp.exp(w_log)
    a = jax.nn.sigmoid(a0_ref[...] + _dot(xa.astype(BF16), aup_ref[...]))
    g = _dot(jax.nn.sigmoid(xg).astype(BF16), gup_ref[...])

    li = lax.broadcasted_iota(jnp.int32, (2 * LANES, LANES), 0)
    lj = lax.broadcasted_iota(jnp.int32, (2 * LANES, LANES), 1)
    head_sum = jnp.where((li & (LANES - 1)) // HEAD_DIM == lj // HEAD_DIM, 1.0, 0.0).astype(BF16)

    kk = k * kk_ref[...]
    kk = kk / jnp.maximum(jnp.sqrt(_dot_hilo(kk * kk, head_sum)), 1e-12)
    kmod = k * (1.0 + (a - 1.0) * ka_ref[...])
    bonus = _dot_hilo(r * kmod * rk_ref[...], head_sum) * v
    kka = kk * a

    lane = lax.broadcasted_iota(jnp.int32, (C, LANES), 1)
    head0 = lane < HEAD_DIM

    def stack(x):
        return jnp.concatenate([jnp.where(head0, x, 0.0), jnp.where(head0, 0.0, x)], axis=0)

    def unstack(x):
        return x[:C] + x[C:]

    ci = lax.broadcasted_iota(jnp.int32, (tc, 2 * tc), 0)
    cj = lax.broadcasted_iota(jnp.int32, (tc, 2 * tc), 1) & (tc - 1)
    cum_mat = jnp.where((cj <= ci) & (cj // C == ci // C), 1.0, 0.0).astype(BF16)
    cum = _dot(cum_mat, jnp.concatenate(_split_bf16(lw), axis=0))
    e_cum = jnp.exp(cum)
    e_inv = jnp.exp(-cum)
    rt_all = r * e_cum
    bt_all = kk * jnp.exp(cum - lw)
    at_all = -kka * e_inv
    kt_all = kmod * e_inv

    gi = lax.broadcasted_iota(jnp.int32, (2 * C2, 2 * C2), 0)
    gj = lax.broadcasted_iota(jnp.int32, (2 * C2, 2 * C2), 1)
    ti, tj = gi & (C - 1), gj & (C - 1)
    gram_mask = tj < ti + jnp.where(gi < C2, 0, 1)
    si = lax.broadcasted_iota(jnp.int32, (C2, C2), 0)
    sj = lax.broadcasted_iota(jnp.int32, (C2, C2), 1)
    eye = si == sj
    s_mask = (si // C) == (sj // C)

    zeros_c = jnp.zeros((C, LANES), F32)

    chunks = range(nchunk)
    sl = [slice(c * C, (c + 1) * C) for c in chunks]
    rt = [rt_all[s] for s in sl]
    bt = [bt_all[s] for s in sl]
    at = [at_all[s] for s in sl]
    kt = [kt_all[s] for s in sl]
    vc = [v[s] for s in sl]
    p_last = [e_cum[(c + 1) * C - 1:(c + 1) * C, :] for c in chunks]

    gram = []
    for c in chunks:
        lhs = jnp.concatenate([stack(bt[c]), stack(rt[c])], axis=0).astype(BF16)
        rhs = jnp.concatenate([stack(at[c]), stack(kt[c])], axis=0).astype(BF16)
        gram.append(jnp.where(gram_mask, _dot_nt(lhs, rhs), 0.0))
    n_ab = [gm[:C2, :C2] for gm in gram]
    a_ra = [gm[C2:, :C2].astype(BF16) for gm in gram]
    mv = [_dot(gram[c][:, C2:].astype(BF16), stack(vc[c]).astype(BF16)) for c in chunks]
    x0 = [unstack(m[:C2]) for m in mv]
    y_rk = [unstack(m[C2:]) for m in mv]

    tinv = [jnp.where(eye, 1.0, n) for n in n_ab]
    pw = [_dot(n.astype(BF16), n.astype(BF16)) for n in n_ab]
    for lvl in range(2, int(math.log2(C)) + 1):
        last = lvl == int(math.log2(C))
        rhs = [tinv[c].astype(BF16) if last else
               jnp.concatenate([tinv[c], pw[c]], axis=1).astype(BF16) for c in chunks]
        prod = [_dot(pw[c].astype(BF16), rhs[c]) for c in chunks]
        tinv = [tinv[c] + prod[c][:, :C2] for c in chunks]
        if not last:
            pw = [prod[c][:, C2:] for c in chunks]

    tb = [_dot(tinv[c].astype(BF16),
               jnp.concatenate([stack(bt[c]), stack(x0[c])], axis=1).astype(BF16)) for c in chunks]
    w_c = [unstack(m[:, :LANES]) for m in tb]
    u0 = [unstack(m[:, LANES:]) for m in tb]
    ab = [_dot(a_ra[c], jnp.concatenate([stack(w_c[c]), stack(u0[c])], axis=1).astype(BF16))
          for c in chunks]
    r_eff = [(rt[c] + unstack(ab[c][:, :LANES])).astype(BF16) for c in chunks]
    y0 = [unstack(ab[c][:, LANES:]) + y_rk[c] for c in chunks]
    tn = [_dot_tn(jnp.concatenate([jnp.concatenate([w_c[c], u0[c]], axis=1),
                                   jnp.concatenate([zeros_c, vc[c]], axis=1)], axis=0).astype(BF16),
                  jnp.concatenate([at[c], kt[c]], axis=0).astype(BF16)) for c in chunks]
    a_c = [(jnp.where(s_mask, m[:C2], 0.0) * p_last[c]).astype(BF16) for c, m in enumerate(tn)]
    b_c = [jnp.where(s_mask, m[C2:], 0.0) * p_last[c] for c, m in enumerate(tn)]

    s = s_ref[...]
    ys = []
    for c in chunks:
        sb = s.astype(BF16)
        ys.append(y0[c] + _dot_nt(r_eff[c], sb))
        s = s * p_last[c] + _dot(sb, a_c[c]) + b_c[c]
    s_ref[...] = s

    y = jnp.concatenate(ys, axis=0)
    mean = _dot_hilo(y, head_sum) * (1.0 / HEAD_DIM)
    yc = y - mean
    var = _dot_hilo(yc * yc, head_sum) * (1.0 / HEAD_DIM)
    yn = yc * lax.rsqrt(var + LNX_EPS) * lnw_ref[...] + lnb_ref[...]
    o_ref[...] = ((yn + bonus) * g).astype(o_ref.dtype)


def _rwkv(z, mu, w_up, w0, a_up, a0, g_up, k_k, k_a, r_k, lnx_w, lnx_b, *, batch, seq, tc):
    n = z.shape[0]
    nt = seq // tc
    pairs = RW_WIDTH // LANES
    col0 = 3 * SB_WIDTH // LANES
    lo_blk = (3 * SB_WIDTH + 3 * RW_WIDTH) // LORA_PAD
    row = lambda b, p, t: (b * nt + t, 0)
    vec = lambda off: pl.BlockSpec((1, LANES), lambda b, p, t: (0, off + p))
    return pl.pallas_call(
        functools.partial(_rwkv_kernel, tc=tc),
        grid=(batch, pairs, nt),
        in_specs=[
            pl.BlockSpec((tc, LANES), lambda b, p, t: (b * nt + t, col0 + p)),
            pl.BlockSpec((tc, LANES), lambda b, p, t: (b * nt + t, col0 + pairs + p)),
            pl.BlockSpec((tc, LANES), lambda b, p, t: (b * nt + t, col0 + 2 * pairs + p)),
            pl.BlockSpec((tc, LORA_PAD), lambda b, p, t: (b * nt + t, lo_blk)),
            vec(0), vec(pairs), vec(2 * pairs),
            pl.BlockSpec((1, LORA_PAD), lambda b, p, t: (0, 3 * RW_WIDTH // LORA_PAD)),
            pl.BlockSpec((DECAY_LORA, LANES), lambda b, p, t: (0, p)),
            vec(0),
            pl.BlockSpec((ICL_LORA, LANES), lambda b, p, t: (0, p)),
            vec(0),
            pl.BlockSpec((LORA_PAD - DECAY_LORA - ICL_LORA, LANES), lambda b, p, t: (0, p)),
            vec(0), vec(0), vec(0), vec(0), vec(0),
        ],
        out_specs=pl.BlockSpec((tc, LANES), lambda b, p, t: (b * nt + t, p)),
        out_shape=jax.ShapeDtypeStruct((n, RW_WIDTH), BF16),
        scratch_shapes=[
            pltpu.VMEM((LANES, LANES), F32),
            pltpu.VMEM((8, LANES), F32), pltpu.VMEM((8, LANES), F32), pltpu.VMEM((8, LANES), F32),
            pltpu.VMEM((8, LORA_PAD), F32),
        ],
        compiler_params=_params("parallel", "parallel", "arbitrary"),
        name="rwkv7",
    )(z, z, z, z, mu, mu, mu, mu, w_up, w0, a_up, a0, g_up, k_k, k_a, r_k, lnx_w, lnx_b)


def _sgu_kernel(x_ref, u_ref, v_ref, lg_ref, lb_ref, ws_ref, bs_ref, wo_ref, o_ref, y_ref, *, tm):
    v = v_ref[...]
    mean = jnp.mean(v, axis=-1, keepdims=True)
    vc = v - mean
    var = jnp.mean(vc * vc, axis=-1, keepdims=True)
    vn = (vc * lax.rsqrt(var + LN_EPS) * lg_ref[...] + lb_ref[...]).astype(BF16)

    ti = lax.broadcasted_iota(jnp.int32, (SGU_BLOCK, SGU_BLOCK), 0) // STREAM_CHUNK
    si = lax.broadcasted_iota(jnp.int32, (SGU_BLOCK, SGU_BLOCK), 1) // STREAM_CHUNK
    causal = si <= ti
    gw = v.shape[1] // SGU_GROUPS
    for g in range(SGU_GROUPS):
        ws = jnp.where(causal, ws_ref[g], 0.0).astype(BF16)
        cols = slice(g * gw, (g + 1) * gw)
        for blk in range(tm // SGU_BLOCK):
            rows = slice(blk * SGU_BLOCK, (blk + 1) * SGU_BLOCK)
            mixed = _dot(ws, vn[rows, cols]) + bs_ref[:, g:g + 1]
            y_ref[rows, cols] = (u_ref[rows, cols] * mixed).astype(BF16)
    o_ref[...] = x_ref[...] + _dot(y_ref[...], wo_ref[...])


def _sgu(x, zc, ln_g, ln_b, w_s, b_s_t, w_out, *, tm):
    n, d = x.shape
    return pl.pallas_call(
        functools.partial(_sgu_kernel, tm=tm),
        grid=(n // tm,),
        in_specs=[
            pl.BlockSpec((tm, d), lambda i: (i, 0)),
            pl.BlockSpec((tm, d), lambda i: (i, 0)),
            pl.BlockSpec((tm, d), lambda i: (i, 1)),
            pl.BlockSpec((1, d), lambda i: (0, 0)),
            pl.BlockSpec((1, d), lambda i: (0, 0)),
            pl.BlockSpec((SGU_GROUPS, SGU_BLOCK, SGU_BLOCK), lambda i: (0, 0, 0)),
            pl.BlockSpec((SGU_BLOCK, SGU_GROUPS), lambda i: (0, 0)),
            pl.BlockSpec((d, d), lambda i: (0, 0)),
        ],
        out_specs=pl.BlockSpec((tm, d), lambda i: (i, 0)),
        out_shape=jax.ShapeDtypeStruct((n, d), F32),
        scratch_shapes=[pltpu.VMEM((tm, d), BF16)],
        compiler_params=_params("parallel"),
        name="sgu",
    )(x, zc, zc, ln_g, ln_b, w_s, b_s_t, w_out)


def _pad_cols(w, width):
    return jnp.pad(w, ((0, 0), (0, width - w.shape[1])))


def _pad_rows(w, height):
    return jnp.pad(w, ((0, height - w.shape[0]), (0, 0)))


def kernel(x, norm_mix, norm_ffn, norm_final, even_w_in, even_mu, even_w_up, even_w0, even_a_up, even_a0, even_g_up, even_k_k, even_k_a, even_r_k, even_lnx_w, even_lnx_b, even_w_out, odd_w_in, odd_ln_g, odd_ln_b, odd_w_s, odd_b_s, odd_w_out, ffn_w1, ffn_w2):
    batch, seq, d = x.shape
    depth = norm_mix.shape[0]
    xf = x.reshape(batch * seq, d)
    row = lambda a: a.reshape(1, -1)

    rw_cols = 3 * RW_WIDTH + LORA_PAD
    in_cols = 3 * SB_WIDTH + rw_cols

    for layer in range(depth):
        i = layer // 2
        if layer % 2 == 0:
            w_in = _pad_cols(even_w_in[i], in_cols).astype(BF16)
            z = _norm_matmul(xf, row(norm_mix[layer]), w_in, tm=512, tn=in_cols // 3)
            o_sb = _sb_attention(z, batch=batch, seq=seq, tq=256)
            o_rw = _rwkv(z, _pad_cols(row(even_mu[i]), rw_cols),
                         even_w_up[i].astype(BF16), row(even_w0[i]),
                         even_a_up[i].astype(BF16), row(even_a0[i]),
                         _pad_rows(even_g_up[i], LORA_PAD - DECAY_LORA - ICL_LORA).astype(BF16),
                         row(even_k_k[i]), row(even_k_a[i]),
                         row(even_r_k[i]), row(even_lnx_w[i]), row(even_lnx_b[i]),
                         batch=batch, seq=seq, tc=512)
            xf = _proj2(xf, o_sb, o_rw, even_w_out[i].astype(BF16), tm=512)
        else:
            zc = _norm_matmul(xf, row(norm_mix[layer]), odd_w_in[i].astype(BF16),
                              tm=512, tn=1024, gelu=True)
            xf = _sgu(xf, zc, row(odd_ln_g[i]), row(odd_ln_b[i]), odd_w_s[i],
                      odd_b_s[i].T, odd_w_out[i].astype(BF16), tm=256)
        xf = _ffn(xf, row(norm_ffn[layer]), ffn_w1[layer].astype(BF16), ffn_w2[layer].astype(BF16),
                  row(norm_final), tm=512, th=1024, final_norm=(layer == depth - 1))
    return xf.reshape(batch, seq, d)
```

```python
import functools
import math

import jax
import jax.numpy as jnp
from jax import lax
from jax.experimental import pallas as pl
from jax.experimental.pallas import tpu as pltpu

F32 = jnp.float32
BF16 = jnp.bfloat16

LANES = 128
HEAD_DIM = 64
RMS_EPS = 1e-6
LN_EPS = 1e-5
LNX_EPS = 64e-5
SB_WIDTH = 1024
RW_WIDTH = 1024
DECAY_LORA = 64
ICL_LORA = 64
GATE_LORA = 160
LORA_PAD = 384
SGU_GROUPS = 8
SGU_BLOCK = 128
STREAM_CHUNK = 64
RW_CHUNK = 64
RW_PAIRS_PER_STEP = 2
VMEM_LIMIT = 56 * 1024 * 1024


def _params(*sem):
    return pltpu.CompilerParams(dimension_semantics=sem, vmem_limit_bytes=VMEM_LIMIT)


def _rms(x, g):
    return x * lax.rsqrt(jnp.mean(x * x, axis=-1, keepdims=True) + RMS_EPS) * g


def _dot(a, b):
    return jnp.dot(a, b, preferred_element_type=F32)


def _dot_nt(a, b):
    return lax.dot_general(a, b, (((1,), (1,)), ((), ())), preferred_element_type=F32)


def _dot_tn(a, b):
    return lax.dot_general(a, b, (((0,), (0,)), ((), ())), preferred_element_type=F32)


def _split_bf16(x):
    hi = x.astype(BF16)
    lo = (x - hi.astype(F32)).astype(BF16)
    return hi, lo


def _dot_hilo(x, w2):
    hi, lo = _split_bf16(x)
    return _dot(jnp.concatenate([hi, lo], axis=1), w2)


def _norm_matmul_kernel(x_ref, g_ref, w_ref, o_ref, h_ref, *, gelu):
    @pl.when(pl.program_id(1) == 0)
    def _():
        h_ref[...] = _rms(x_ref[...], g_ref[...]).astype(BF16)

    y = _dot(h_ref[...], w_ref[...])
    if gelu:
        y = 0.5 * y * (1.0 + lax.erf(y * math.sqrt(0.5)))
    o_ref[...] = y.astype(o_ref.dtype)


def _norm_matmul(x, g, w, *, tm, tn, gelu=False, out_dtype=F32):
    n, d = x.shape
    nout = w.shape[1]
    return pl.pallas_call(
        functools.partial(_norm_matmul_kernel, gelu=gelu),
        grid=(n // tm, nout // tn),
        in_specs=[
            pl.BlockSpec((tm, d), lambda i, j: (i, 0)),
            pl.BlockSpec((1, d), lambda i, j: (0, 0)),
            pl.BlockSpec((d, tn), lambda i, j: (0, j)),
        ],
        out_specs=pl.BlockSpec((tm, tn), lambda i, j: (i, j)),
        out_shape=jax.ShapeDtypeStruct((n, nout), out_dtype),
        scratch_shapes=[pltpu.VMEM((tm, d), BF16)],
        compiler_params=_params("parallel", "arbitrary"),
        name="norm_matmul_gelu" if gelu else "norm_matmul",
    )(x, g, w)


def _ffn_kernel(x_ref, g_ref, w1_ref, w2_ref, gf_ref, o_ref, h_ref, *, final_norm):
    j = pl.program_id(1)

    @pl.when(j == 0)
    def _():
        x = x_ref[...]
        h_ref[...] = _rms(x, g_ref[...]).astype(BF16)
        o_ref[...] = x

    a = jnp.square(jnp.maximum(_dot(h_ref[...], w1_ref[...]), 0.0)).astype(BF16)
    o_ref[...] += _dot(a, w2_ref[...])

    if final_norm:
        @pl.when(j == pl.num_programs(1) - 1)
        def _():
            o_ref[...] = _rms(o_ref[...], gf_ref[...])


def _ffn(x, g, w1, w2, gf, *, tm, th, final_norm):
    n, d = x.shape
    hid = w1.shape[1]
    return pl.pallas_call(
        functools.partial(_ffn_kernel, final_norm=final_norm),
        grid=(n // tm, hid // th),
        in_specs=[
            pl.BlockSpec((tm, d), lambda i, j: (i, 0)),
            pl.BlockSpec((1, d), lambda i, j: (0, 0)),
            pl.BlockSpec((d, th), lambda i, j: (0, j)),
            pl.BlockSpec((th, d), lambda i, j: (j, 0)),
            pl.BlockSpec((1, d), lambda i, j: (0, 0)),
        ],
        out_specs=pl.BlockSpec((tm, d), lambda i, j: (i, 0)),
        out_shape=jax.ShapeDtypeStruct((n, d), F32),
        scratch_shapes=[pltpu.VMEM((tm, d), BF16)],
        compiler_params=_params("parallel", "arbitrary"),
        name="ffn_final" if final_norm else "ffn",
    )(x, g, w1, w2, gf)


def _proj2_kernel(x_ref, a_ref, b_ref, w_ref, o_ref):
    ka = a_ref.shape[1]
    o_ref[...] = (x_ref[...] + _dot(a_ref[...], w_ref[:ka, :]) + _dot(b_ref[...], w_ref[ka:, :]))


def _proj2(x, a, b, w, *, tm):
    n, d = x.shape
    ka, kb = a.shape[1], b.shape[1]
    return pl.pallas_call(
        _proj2_kernel,
        grid=(n // tm,),
        in_specs=[
            pl.BlockSpec((tm, d), lambda i: (i, 0)),
            pl.BlockSpec((tm, ka), lambda i: (i, 0)),
            pl.BlockSpec((tm, kb), lambda i: (i, 0)),
            pl.BlockSpec((ka + kb, d), lambda i: (0, 0)),
        ],
        out_specs=pl.BlockSpec((tm, d), lambda i: (i, 0)),
        out_shape=jax.ShapeDtypeStruct((n, d), F32),
        compiler_params=_params("parallel"),
        name="out_proj",
    )(x, a, b, w)


NEG_BIG = -1e30
SB_KEY_TILE = 128
DEAD_LOG_WEIGHT = -104.0


def _sb_kernel(q_ref, k_ref, v_ref, o_ref, z_ref, hilo_ref, ls_ref, e_ref, rs_ref, c_ref, acc_ref, *, tq):
    tk = SB_KEY_TILE
    per_q = tq // tk
    assert per_q == 2
    qi = pl.program_id(2)
    n_tiles = (qi + 1) * per_q
    lane = lax.broadcasted_iota(jnp.int32, (tq, LANES), 1)
    head0 = lane < HEAD_DIM
    q = q_ref[...] * (HEAD_DIM ** -0.5)
    qh = (jnp.where(head0, q, 0.0).astype(BF16), jnp.where(head0, 0.0, q).astype(BF16))

    jj = lax.broadcasted_iota(jnp.int32, (2 * tk, 2 * tk), 0)
    ss = lax.broadcasted_iota(jnp.int32, (2 * tk, 2 * tk), 1)
    tail_mat = jnp.where((ss >= tk) | ((jj & (tk - 1)) > ss), 1.0, 0.0).astype(BF16)

    def key_tile(m):
        return jnp.maximum(n_tiles - 1 - m, 0)

    def tile_rows(m):
        return pl.ds(pl.multiple_of(key_tile(m) * tk, tk), tk)

    def stage_a1(m, slot):
        kb = k_ref[tile_rows(m), :].astype(BF16)
        for h in range(2):
            z_ref[h, slot] = _dot_nt(qh[h], kb)

    def stage_a2(slot, masked_tile=None):
        for h in range(2):
            z = z_ref[h, slot]
            sp = jnp.maximum(z, 0.0) + jnp.log(1.0 + jnp.exp(-jnp.abs(z)))
            log_keep = -sp
            log_beta = z - sp
            if masked_tile is not None:
                t_pos = qi * tq + lax.broadcasted_iota(jnp.int32, (tq, tk), 0)
                s_pos = key_tile(masked_tile) * tk + lax.broadcasted_iota(jnp.int32, (tq, tk), 1)
                log_keep = jnp.where(s_pos < t_pos, log_keep, 0.0)
                log_beta = jnp.where(s_pos < t_pos, log_beta, NEG_BIG)
            hi, lo = _split_bf16(log_keep)
            hilo_ref[h, slot, :, :tk] = hi
            hilo_ref[h, slot, :, tk:] = lo
            ls_ref[h, slot] = log_beta

    def stage_b(slot):
        for h in range(2):
            both = _dot(hilo_ref[h, slot], tail_mat)
            e_ref[h, slot] = ls_ref[h, slot] + both[:, :tk]
            rs_ref[h, slot] = both[:, tk:]

    def stage_c(m, slot):
        vb = v_ref[tile_rows(m), :].astype(BF16)
        for h in range(2):
            c = c_ref[h]
            w = jnp.exp(e_ref[h, slot] + c)
            acc_ref[h] += _dot(w.astype(BF16), vb)
            c_ref[h] = c + rs_ref[h, slot]

    c_ref[...] = jnp.zeros_like(c_ref)
    acc_ref[...] = jnp.zeros_like(acc_ref)

    stage_a1(0, 0)
    stage_a2(0, masked_tile=0)
    stage_a1(1, 1)
    stage_b(0)
    stage_a2(1, masked_tile=1 if per_q > 1 else None)
    stage_a1(2, 0)

    def two_steps(state):
        pair, _ = state
        i = 2 * pair + 3
        for s_new, s_old in ((1, 0), (0, 1)):
            stage_a1(i, s_new)
            stage_b(s_new)
            stage_c(i - 3, s_old)
            stage_a2(s_old)
            i = i + 1
        return pair + 1, jnp.max(c_ref[...]) >= DEAD_LOG_WEIGHT

    lax.while_loop(lambda state: (state[0] < n_tiles // 2) & state[1], two_steps,
                   (jnp.int32(0), jnp.bool_(True)))

    o_ref[...] = jnp.where(head0, acc_ref[0], acc_ref[1]).astype(o_ref.dtype)


def _sb_attention(z, *, batch, seq, tq):
    n = z.shape[0]
    nq = seq // tq
    pairs = SB_WIDTH // LANES
    return pl.pallas_call(
        functools.partial(_sb_kernel, tq=tq),
        grid=(batch, pairs, nq),
        in_specs=[
            pl.BlockSpec((tq, LANES), lambda b, p, i: (b * nq + i, p)),
            pl.BlockSpec((seq, LANES), lambda b, p, i: (b, pairs + p)),
            pl.BlockSpec((seq, LANES), lambda b, p, i: (b, 2 * pairs + p)),
        ],
        out_specs=pl.BlockSpec((tq, LANES), lambda b, p, i: (b * nq + i, p)),
        out_shape=jax.ShapeDtypeStruct((n, SB_WIDTH), BF16),
        scratch_shapes=[
            pltpu.VMEM((2, 2, tq, SB_KEY_TILE), F32),
            pltpu.VMEM((2, 2, tq, 2 * SB_KEY_TILE), BF16),
            pltpu.VMEM((2, 2, tq, SB_KEY_TILE), F32),
            pltpu.VMEM((2, 2, tq, SB_KEY_TILE), F32),
            pltpu.VMEM((2, 2, tq, SB_KEY_TILE), F32),
            pltpu.VMEM((2, tq, SB_KEY_TILE), F32),
            pltpu.VMEM((2, tq, LANES), F32),
        ],
        compiler_params=_params("parallel", "parallel", "arbitrary"),
        name="stick_breaking",
    )(z, z, z)


def _rwkv_kernel(r_ref, k_ref, v_ref, lo_ref, mur_ref, muk_ref, muv_ref, mulo_ref,
                 wup_ref, w0_ref, aup_ref, a0_ref, gup_ref, kk_ref, ka_ref, rk_ref,
                 lnw_ref, lnb_ref, o_ref, s_ref, pr_ref, pk_ref, pv_ref, plo_ref, *, tc):
    C = RW_CHUNK
    C2 = 2 * C
    nchunk = tc // C
    npair = RW_PAIRS_PER_STEP
    width = npair * LANES
    t = pl.program_id(2)

    @pl.when(t == 0)
    def _():
        s_ref[...] = jnp.zeros_like(s_ref)
        pr_ref[...] = jnp.zeros_like(pr_ref)
        pk_ref[...] = jnp.zeros_like(pk_ref)
        pv_ref[...] = jnp.zeros_like(pv_ref)
        plo_ref[...] = jnp.zeros_like(plo_ref)

    def shifted(z_ref, prev_ref, mu_ref):
        z = z_ref[...]
        row = lax.broadcasted_iota(jnp.int32, z.shape, 0)
        prev = jnp.where(row == 0, prev_ref[0:1, :], pltpu.roll(z, 1, axis=0))
        prev_ref[0:1, :] = z[tc - 1:tc, :]
        return z + (prev - z) * mu_ref[...]

    r = shifted(r_ref, pr_ref, mur_ref)
    k = shifted(k_ref, pk_ref, muk_ref)
    v = shifted(v_ref, pv_ref, muv_ref)
    lo = shifted(lo_ref, plo_ref, mulo_ref)
    xw = lo[:, :DECAY_LORA]
    xa = lo[:, DECAY_LORA:DECAY_LORA + ICL_LORA]
    xg = lo[:, DECAY_LORA + ICL_LORA:]

    wl = w0_ref[...] + _dot(jnp.tanh(xw).astype(BF16), wup_ref[...])
    w_log = -(jnp.maximum(-wl, 0.0) + jnp.log1p(jnp.exp(-jnp.abs(wl)))) - 0.5
    lw = -jnp.exp(w_log)
    a = jax.nn.sigmoid(a0_ref[...] + _dot(xa.astype(BF16), aup_ref[...]))
    g = _dot(jax.nn.sigmoid(xg).astype(BF16), gup_ref[...])

    li = lax.broadcasted_iota(jnp.int32, (2 * width, width), 0)
    lj = lax.broadcasted_iota(jnp.int32, (2 * width, width), 1)
    head_sum = jnp.where((li & (width - 1)) // HEAD_DIM == lj // HEAD_DIM, 1.0, 0.0).astype(BF16)

    kk = k * kk_ref[...]
    kk = kk / jnp.maximum(jnp.sqrt(_dot_hilo(kk * kk, head_sum)), 1e-12)
    kmod = k * (1.0 + (a - 1.0) * ka_ref[...])
    bonus = _dot_hilo(r * kmod * rk_ref[...], head_sum) * v
    kka = kk * a

    lane = lax.broadcasted_iota(jnp.int32, (C, LANES), 1)
    head0 = lane < HEAD_DIM

    def stack(x):
        return jnp.concatenate([jnp.where(head0, x, 0.0), jnp.where(head0, 0.0, x)], axis=0)

    def unstack(x):
        return x[:C] + x[C:]

    ci = lax.broadcasted_iota(jnp.int32, (C, C2), 0)
    cj = lax.broadcasted_iota(jnp.int32, (C, C2), 1) & (C - 1)
    cum_mat = jnp.where(cj <= ci, 1.0, 0.0).astype(BF16)
    lw_hi, lw_lo = _split_bf16(lw)
    cum = jnp.concatenate(
        [_dot(cum_mat, jnp.concatenate([lw_hi[c * C:(c + 1) * C], lw_lo[c * C:(c + 1) * C]], axis=0))
         for c in range(nchunk)], axis=0)
    e_cum = jnp.exp(cum)
    e_inv = jnp.exp(-cum)
    rt_all = r * e_cum
    bt_all = kk * jnp.exp(cum - lw)
    at_all = -kka * e_inv
    kt_all = kmod * e_inv

    gi = lax.broadcasted_iota(jnp.int32, (2 * C2, 2 * C2), 0)
    gj = lax.broadcasted_iota(jnp.int32, (2 * C2, 2 * C2), 1)
    ti, tj = gi & (C - 1), gj & (C - 1)
    gram_mask = tj < ti + jnp.where(gi < C2, 0, 1)
    si = lax.broadcasted_iota(jnp.int32, (C2, C2), 0)
    sj = lax.broadcasted_iota(jnp.int32, (C2, C2), 1)
    eye = si == sj
    s_mask = (si // C) == (sj // C)

    zeros_c = jnp.zeros((C, LANES), F32)

    probs = [(c, p) for c in range(nchunk) for p in range(npair)]
    chunks = range(len(probs))
    sl = [(slice(c * C, (c + 1) * C), slice(p * LANES, (p + 1) * LANES)) for c, p in probs]
    rt = [rt_all[s] for s in sl]
    bt = [bt_all[s] for s in sl]
    at = [at_all[s] for s in sl]
    kt = [kt_all[s] for s in sl]
    vc = [v[s] for s in sl]
    p_last = [e_cum[(c + 1) * C - 1:(c + 1) * C, p * LANES:(p + 1) * LANES] for c, p in probs]

    gram = []
    for c in chunks:
        lhs = jnp.concatenate([stack(bt[c]), stack(rt[c])], axis=0).astype(BF16)
        rhs = jnp.concatenate([stack(at[c]), stack(kt[c])], axis=0).astype(BF16)
        gram.append(jnp.where(gram_mask, _dot_nt(lhs, rhs), 0.0))
    n_ab = [gm[:C2, :C2] for gm in gram]
    a_ra = [gm[C2:, :C2].astype(BF16) for gm in gram]
    mv = [_dot(gram[c][:, C2:].astype(BF16), stack(vc[c]).astype(BF16)) for c in chunks]
    x0 = [unstack(m[:C2]) for m in mv]
    y_rk = [unstack(m[C2:]) for m in mv]

    tinv = [jnp.where(eye, 1.0, n) for n in n_ab]
    pw = [_dot(n.astype(BF16), n.astype(BF16)) for n in n_ab]
    for lvl in range(2, int(math.log2(C)) + 1):
        last = lvl == int(math.log2(C))
        rhs = [tinv[c].astype(BF16) if last else
               jnp.concatenate([tinv[c], pw[c]], axis=1).astype(BF16) for c in chunks]
        prod = [_dot(pw[c].astype(BF16), rhs[c]) for c in chunks]
        tinv = [tinv[c] + prod[c][:, :C2] for c in chunks]
        if not last:
            pw = [prod[c][:, C2:] for c in chunks]

    tb = [_dot(tinv[c].astype(BF16),
               jnp.concatenate([stack(bt[c]), stack(x0[c])], axis=1).astype(BF16)) for c in chunks]
    w_c = [unstack(m[:, :LANES]) for m in tb]
    u0 = [unstack(m[:, LANES:]) for m in tb]
    ab = [_dot(a_ra[c], jnp.concatenate([stack(w_c[c]), stack(u0[c])], axis=1).astype(BF16))
          for c in chunks]
    r_eff = [(rt[c] + unstack(ab[c][:, :LANES])).astype(BF16) for c in chunks]
    y0 = [unstack(ab[c][:, LANES:]) + y_rk[c] for c in chunks]
    tn = [_dot_tn(jnp.concatenate([jnp.concatenate([w_c[c], u0[c]], axis=1),
                                   jnp.concatenate([zeros_c, vc[c]], axis=1)], axis=0).astype(BF16),
                  jnp.concatenate([at[c], kt[c]], axis=0).astype(BF16)) for c in chunks]
    a_c = [(jnp.where(s_mask, m[:C2], 0.0) * p_last[c]).astype(BF16) for c, m in enumerate(tn)]
    b_c = [jnp.where(s_mask, m[C2:], 0.0) * p_last[c] for c, m in enumerate(tn)]

    s = [s_ref[p] for p in range(npair)]
    ys = []
    for i, (c, p) in enumerate(probs):
        sb = s[p].astype(BF16)
        ys.append(y0[i] + _dot_nt(r_eff[i], sb))
        s[p] = s[p] * p_last[i] + _dot(sb, a_c[i]) + b_c[i]
    for p in range(npair):
        s_ref[p] = s[p]

    y = jnp.concatenate([jnp.concatenate(ys[c * npair:(c + 1) * npair], axis=1)
                         for c in range(nchunk)], axis=0)
    mean = _dot_hilo(y, head_sum) * (1.0 / HEAD_DIM)
    yc = y - mean
    var = _dot_hilo(yc * yc, head_sum) * (1.0 / HEAD_DIM)
    yn = yc * lax.rsqrt(var + LNX_EPS) * lnw_ref[...] + lnb_ref[...]
    o_ref[...] = ((yn + bonus) * g).astype(o_ref.dtype)


def _rwkv(z, mu, w_up, w0, a_up, a0, g_up, k_k, k_a, r_k, lnx_w, lnx_b, *, batch, seq, tc):
    n = z.shape[0]
    nt = seq // tc
    width = RW_PAIRS_PER_STEP * LANES
    groups = RW_WIDTH // width
    col0 = 3 * SB_WIDTH // width
    lo_blk = (3 * SB_WIDTH + 3 * RW_WIDTH) // LORA_PAD
    vec = lambda off: pl.BlockSpec((1, width), lambda b, p, t: (0, off + p))
    return pl.pallas_call(
        functools.partial(_rwkv_kernel, tc=tc),
        grid=(batch, groups, nt),
        in_specs=[
            pl.BlockSpec((tc, width), lambda b, p, t: (b * nt + t, col0 + p)),
            pl.BlockSpec((tc, width), lambda b, p, t: (b * nt + t, col0 + groups + p)),
            pl.BlockSpec((tc, width), lambda b, p, t: (b * nt + t, col0 + 2 * groups + p)),
            pl.BlockSpec((tc, LORA_PAD), lambda b, p, t: (b * nt + t, lo_blk)),
            vec(0), vec(groups), vec(2 * groups),
            pl.BlockSpec((1, LORA_PAD), lambda b, p, t: (0, 3 * RW_WIDTH // LORA_PAD)),
            pl.BlockSpec((DECAY_LORA, width), lambda b, p, t: (0, p)),
            vec(0),
            pl.BlockSpec((ICL_LORA, width), lambda b, p, t: (0, p)),
            vec(0),
            pl.BlockSpec((LORA_PAD - DECAY_LORA - ICL_LORA, width), lambda b, p, t: (0, p)),
            vec(0), vec(0), vec(0), vec(0), vec(0),
        ],
        out_specs=pl.BlockSpec((tc, width), lambda b, p, t: (b * nt + t, p)),
        out_shape=jax.ShapeDtypeStruct((n, RW_WIDTH), BF16),
        scratch_shapes=[
            pltpu.VMEM((RW_PAIRS_PER_STEP, LANES, LANES), F32),
            pltpu.VMEM((8, width), F32), pltpu.VMEM((8, width), F32), pltpu.VMEM((8, width), F32),
            pltpu.VMEM((8, LORA_PAD), F32),
        ],
        compiler_params=_params("parallel", "parallel", "arbitrary"),
        name="rwkv7",
    )(z, z, z, z, mu, mu, mu, mu, w_up, w0, a_up, a0, g_up, k_k, k_a, r_k, lnx_w, lnx_b)


def _sgu_kernel(x_ref, u_ref, v_ref, lg_ref, lb_ref, ws_ref, bs_ref, wo_ref, o_ref, y_ref, *, tm):
    v = v_ref[...]
    mean = jnp.mean(v, axis=-1, keepdims=True)
    vc = v - mean
    var = jnp.mean(vc * vc, axis=-1, keepdims=True)
    vn = (vc * lax.rsqrt(var + LN_EPS) * lg_ref[...] + lb_ref[...]).astype(BF16)

    ti = lax.broadcasted_iota(jnp.int32, (SGU_BLOCK, SGU_BLOCK), 0) // STREAM_CHUNK
    si = lax.broadcasted_iota(jnp.int32, (SGU_BLOCK, SGU_BLOCK), 1) // STREAM_CHUNK
    causal = si <= ti
    gw = v.shape[1] // SGU_GROUPS
    for g in range(SGU_GROUPS):
        ws = jnp.where(causal, ws_ref[g], 0.0).astype(BF16)
        cols = slice(g * gw, (g + 1) * gw)
        for blk in range(tm // SGU_BLOCK):
            rows = slice(blk * SGU_BLOCK, (blk + 1) * SGU_BLOCK)
            mixed = _dot(ws, vn[rows, cols]) + bs_ref[:, g:g + 1]
            y_ref[rows, cols] = (u_ref[rows, cols] * mixed).astype(BF16)
    o_ref[...] = x_ref[...] + _dot(y_ref[...], wo_ref[...])


def _sgu(x, zc, ln_g, ln_b, w_s, b_s_t, w_out, *, tm):
    n, d = x.shape
    return pl.pallas_call(
        functools.partial(_sgu_kernel, tm=tm),
        grid=(n // tm,),
        in_specs=[
            pl.BlockSpec((tm, d), lambda i: (i, 0)),
            pl.BlockSpec((tm, d), lambda i: (i, 0)),
            pl.BlockSpec((tm, d), lambda i: (i, 1)),
            pl.BlockSpec((1, d), lambda i: (0, 0)),
            pl.BlockSpec((1, d), lambda i: (0, 0)),
            pl.BlockSpec((SGU_GROUPS, SGU_BLOCK, SGU_BLOCK), lambda i: (0, 0, 0)),
            pl.BlockSpec((SGU_BLOCK, SGU_GROUPS), lambda i: (0, 0)),
            pl.BlockSpec((d, d), lambda i: (0, 0)),
        ],
        out_specs=pl.BlockSpec((tm, d), lambda i: (i, 0)),
        out_shape=jax.ShapeDtypeStruct((n, d), F32),
        scratch_shapes=[pltpu.VMEM((tm, d), BF16)],
        compiler_params=_params("parallel"),
        name="sgu",
    )(x, zc, zc, ln_g, ln_b, w_s, b_s_t, w_out)


def _pad_cols(w, width):
    return jnp.pad(w, ((0, 0), (0, width - w.shape[1])))


def _pad_rows(w, height):
    return jnp.pad(w, ((0, height - w.shape[0]), (0, 0)))


def kernel(x, norm_mix, norm_ffn, norm_final, even_w_in, even_mu, even_w_up, even_w0, even_a_up, even_a0, even_g_up, even_k_k, even_k_a, even_r_k, even_lnx_w, even_lnx_b, even_w_out, odd_w_in, odd_ln_g, odd_ln_b, odd_w_s, odd_b_s, odd_w_out, ffn_w1, ffn_w2):
    batch, seq, d = x.shape
    depth = norm_mix.shape[0]
    xf = x.reshape(batch * seq, d)
    row = lambda a: a.reshape(1, -1)

    rw_cols = 3 * RW_WIDTH + LORA_PAD
    in_cols = 3 * SB_WIDTH + rw_cols

    for layer in range(depth):
        i = layer // 2
        if layer % 2 == 0:
            w_in = _pad_cols(even_w_in[i], in_cols).astype(BF16)
            z = _norm_matmul(xf, row(norm_mix[layer]), w_in, tm=512, tn=in_cols // 3)
            o_sb = _sb_attention(z, batch=batch, seq=seq, tq=256)
            o_rw = _rwkv(z, _pad_cols(row(even_mu[i]), rw_cols),
                         even_w_up[i].astype(BF16), row(even_w0[i]),
                         even_a_up[i].astype(BF16), row(even_a0[i]),
                         _pad_rows(even_g_up[i], LORA_PAD - DECAY_LORA - ICL_LORA).astype(BF16),
                         row(even_k_k[i]), row(even_k_a[i]),
                         row(even_r_k[i]), row(even_lnx_w[i]), row(even_lnx_b[i]),
                         batch=batch, seq=seq, tc=512)
            xf = _proj2(xf, o_sb, o_rw, even_w_out[i].astype(BF16), tm=512)
        else:
            zc = _norm_matmul(xf, row(norm_mix[layer]), odd_w_in[i].astype(BF16),
                              tm=512, tn=1024, gelu=True)
            xf = _sgu(xf, zc, row(odd_ln_g[i]), row(odd_ln_b[i]), odd_w_s[i],
                      odd_b_s[i].T, odd_w_out[i].astype(BF16), tm=256)
        xf = _ffn(xf, row(norm_ffn[layer]), ffn_w1[layer].astype(BF16), ffn_w2[layer].astype(BF16),
                  row(norm_final), tm=512, th=1024, final_norm=(layer == depth - 1))
    return xf.reshape(batch, seq, d)
```

```python
import functools
import math

import jax
import jax.numpy as jnp
from jax import lax
from jax.experimental import pallas as pl
from jax.experimental.pallas import tpu as pltpu

F32 = jnp.float32
BF16 = jnp.bfloat16

LANES = 128
HEAD_DIM = 64
RMS_EPS = 1e-6
LN_EPS = 1e-5
LNX_EPS = 64e-5
SB_WIDTH = 1024
RW_WIDTH = 1024
DECAY_LORA = 64
ICL_LORA = 64
GATE_LORA = 160
LORA_PAD = 384
SGU_GROUPS = 8
SGU_BLOCK = 128
STREAM_CHUNK = 64
RW_CHUNK = 64
RW_PAIRS_PER_STEP = 2
VMEM_LIMIT = 56 * 1024 * 1024


def _params(*sem):
    return pltpu.CompilerParams(dimension_semantics=sem, vmem_limit_bytes=VMEM_LIMIT)


def _rms(x, g):
    return x * lax.rsqrt(jnp.mean(x * x, axis=-1, keepdims=True) + RMS_EPS) * g


def _dot(a, b):
    return jnp.dot(a, b, preferred_element_type=F32)


def _dot_nt(a, b):
    return lax.dot_general(a, b, (((1,), (1,)), ((), ())), preferred_element_type=F32)


def _dot_tn(a, b):
    return lax.dot_general(a, b, (((0,), (0,)), ((), ())), preferred_element_type=F32)


def _split_bf16(x):
    hi = x.astype(BF16)
    lo = (x - hi.astype(F32)).astype(BF16)
    return hi, lo


def _norm_matmul_kernel(x_ref, g_ref, w_ref, o_ref, h_ref, *, gelu):
    @pl.when(pl.program_id(1) == 0)
    def _():
        h_ref[...] = _rms(x_ref[...], g_ref[...]).astype(BF16)

    y = _dot(h_ref[...], w_ref[...])
    if gelu:
        y = 0.5 * y * (1.0 + lax.erf(y * math.sqrt(0.5)))
    o_ref[...] = y.astype(o_ref.dtype)


def _norm_matmul(x, g, w, *, tm, tn, gelu=False, out_dtype=F32):
    n, d = x.shape
    nout = w.shape[1]
    return pl.pallas_call(
        functools.partial(_norm_matmul_kernel, gelu=gelu),
        grid=(n // tm, nout // tn),
        in_specs=[
            pl.BlockSpec((tm, d), lambda i, j: (i, 0)),
            pl.BlockSpec((1, d), lambda i, j: (0, 0)),
            pl.BlockSpec((d, tn), lambda i, j: (0, j)),
        ],
        out_specs=pl.BlockSpec((tm, tn), lambda i, j: (i, j)),
        out_shape=jax.ShapeDtypeStruct((n, nout), out_dtype),
        scratch_shapes=[pltpu.VMEM((tm, d), BF16)],
        compiler_params=_params("parallel", "arbitrary"),
        name="norm_matmul_gelu" if gelu else "norm_matmul",
    )(x, g, w)


def _ffn_kernel(x_ref, g_ref, w1_ref, w2_ref, gf_ref, o_ref, h_ref, *, final_norm):
    j = pl.program_id(1)

    @pl.when(j == 0)
    def _():
        x = x_ref[...]
        h_ref[...] = _rms(x, g_ref[...]).astype(BF16)
        o_ref[...] = x

    a = jnp.square(jnp.maximum(_dot(h_ref[...], w1_ref[...]), 0.0)).astype(BF16)
    o_ref[...] += _dot(a, w2_ref[...])

    if final_norm:
        @pl.when(j == pl.num_programs(1) - 1)
        def _():
            o_ref[...] = _rms(o_ref[...], gf_ref[...])


def _ffn(x, g, w1, w2, gf, *, tm, th, final_norm):
    n, d = x.shape
    hid = w1.shape[1]
    return pl.pallas_call(
        functools.partial(_ffn_kernel, final_norm=final_norm),
        grid=(n // tm, hid // th),
        in_specs=[
            pl.BlockSpec((tm, d), lambda i, j: (i, 0)),
            pl.BlockSpec((1, d), lambda i, j: (0, 0)),
            pl.BlockSpec((d, th), lambda i, j: (0, j)),
            pl.BlockSpec((th, d), lambda i, j: (j, 0)),
            pl.BlockSpec((1, d), lambda i, j: (0, 0)),
        ],
        out_specs=pl.BlockSpec((tm, d), lambda i, j: (i, 0)),
        out_shape=jax.ShapeDtypeStruct((n, d), F32),
        scratch_shapes=[pltpu.VMEM((tm, d), BF16)],
        compiler_params=_params("parallel", "arbitrary"),
        name="ffn_final" if final_norm else "ffn",
    )(x, g, w1, w2, gf)


def _proj2_kernel(x_ref, a_ref, b_ref, w_ref, o_ref):
    ka = a_ref.shape[1]
    o_ref[...] = (x_ref[...] + _dot(a_ref[...], w_ref[:ka, :]) + _dot(b_ref[...], w_ref[ka:, :]))


def _proj2(x, a, b, w, *, tm):
    n, d = x.shape
    ka, kb = a.shape[1], b.shape[1]
    return pl.pallas_call(
        _proj2_kernel,
        grid=(n // tm,),
        in_specs=[
            pl.BlockSpec((tm, d), lambda i: (i, 0)),
            pl.BlockSpec((tm, ka), lambda i: (i, 0)),
            pl.BlockSpec((tm, kb), lambda i: (i, 0)),
            pl.BlockSpec((ka + kb, d), lambda i: (0, 0)),
        ],
        out_specs=pl.BlockSpec((tm, d), lambda i: (i, 0)),
        out_shape=jax.ShapeDtypeStruct((n, d), F32),
        compiler_params=_params("parallel"),
        name="out_proj",
    )(x, a, b, w)


NEG_BIG = -1e30
SB_KEY_TILE = 128
DEAD_LOG2_WEIGHT = -150.5


def _sb_kernel(q_ref, k_ref, v_ref, o_ref, z_ref, hilo_ref, ls_ref, e_ref, rs_ref, c_ref, acc_ref, *, tq):
    tk = SB_KEY_TILE
    per_q = tq // tk
    assert per_q == 2
    qi = pl.program_id(2)
    n_tiles = (qi + 1) * per_q
    lane = lax.broadcasted_iota(jnp.int32, (tq, LANES), 1)
    head0 = lane < HEAD_DIM
    q = q_ref[...] * (HEAD_DIM ** -0.5 * math.log2(math.e))
    qh = (jnp.where(head0, q, 0.0).astype(BF16), jnp.where(head0, 0.0, q).astype(BF16))

    jj = lax.broadcasted_iota(jnp.int32, (2 * tk, 2 * tk), 0)
    ss = lax.broadcasted_iota(jnp.int32, (2 * tk, 2 * tk), 1)
    tail_mat = jnp.where((ss >= tk) | ((jj & (tk - 1)) > ss), -1.0, 0.0).astype(BF16)

    def key_tile(m):
        return jnp.maximum(n_tiles - 1 - m, 0)

    def tile_rows(m):
        return pl.ds(pl.multiple_of(key_tile(m) * tk, tk), tk)

    def stage_a1(m, slot):
        kb = k_ref[tile_rows(m), :].astype(BF16)
        for h in range(2):
            z_ref[h, slot] = _dot_nt(qh[h], kb)

    def stage_a2(slot, masked_tile=None):
        for h in range(2):
            z = z_ref[h, slot]
            sp = jnp.maximum(z, 0.0) + jnp.log2(1.0 + jnp.exp2(-jnp.abs(z)))
            log_beta = z - sp
            if masked_tile is not None:
                t_pos = qi * tq + lax.broadcasted_iota(jnp.int32, (tq, tk), 0)
                s_pos = key_tile(masked_tile) * tk + lax.broadcasted_iota(jnp.int32, (tq, tk), 1)
                sp = jnp.where(s_pos < t_pos, sp, 0.0)
                log_beta = jnp.where(s_pos < t_pos, log_beta, NEG_BIG)
            hi, lo = _split_bf16(sp)
            hilo_ref[h, slot, :, :tk] = hi
            hilo_ref[h, slot, :, tk:] = lo
            ls_ref[h, slot] = log_beta

    def stage_b(slot):
        for h in range(2):
            both = _dot(hilo_ref[h, slot], tail_mat)
            e_ref[h, slot] = ls_ref[h, slot] + both[:, :tk]
            rs_ref[h, slot] = both[:, tk:]

    def stage_c(m, slot):
        vb = v_ref[tile_rows(m), :].astype(BF16)
        for h in range(2):
            c = c_ref[h]
            w = jnp.exp2(e_ref[h, slot] + c)
            acc_ref[h] += _dot(w.astype(BF16), vb)
            c_ref[h] = c + rs_ref[h, slot]

    c_ref[...] = jnp.zeros_like(c_ref)
    acc_ref[...] = jnp.zeros_like(acc_ref)

    stage_a1(0, 0)
    stage_a2(0, masked_tile=0)
    stage_a1(1, 1)
    stage_b(0)
    stage_a2(1, masked_tile=1 if per_q > 1 else None)
    stage_a1(2, 0)

    def two_steps(state):
        pair, _ = state
        i = 2 * pair + 3
        for s_new, s_old in ((1, 0), (0, 1)):
            stage_a1(i, s_new)
            stage_b(s_new)
            stage_c(i - 3, s_old)
            stage_a2(s_old)
            i = i + 1
        return pair + 1, jnp.max(c_ref[...]) >= DEAD_LOG2_WEIGHT

    first = two_steps((0, None))
    lax.while_loop(lambda state: (state[0] < n_tiles // 2) & state[1], two_steps,
                   (jnp.int32(first[0]), first[1]))

    o_ref[...] = jnp.where(head0, acc_ref[0], acc_ref[1]).astype(o_ref.dtype)


def _sb_attention(z, *, batch, seq, tq):
    n = z.shape[0]
    nq = seq // tq
    pairs = SB_WIDTH // LANES
    return pl.pallas_call(
        functools.partial(_sb_kernel, tq=tq),
        grid=(batch, pairs, nq),
        in_specs=[
            pl.BlockSpec((tq, LANES), lambda b, p, i: (b * nq + i, p)),
            pl.BlockSpec((seq, LANES), lambda b, p, i: (b, pairs + p)),
            pl.BlockSpec((seq, LANES), lambda b, p, i: (b, 2 * pairs + p)),
        ],
        out_specs=pl.BlockSpec((tq, LANES), lambda b, p, i: (b * nq + i, p)),
        out_shape=jax.ShapeDtypeStruct((n, SB_WIDTH), BF16),
        scratch_shapes=[
            pltpu.VMEM((2, 2, tq, SB_KEY_TILE), F32),
            pltpu.VMEM((2, 2, tq, 2 * SB_KEY_TILE), BF16),
            pltpu.VMEM((2, 2, tq, SB_KEY_TILE), F32),
            pltpu.VMEM((2, 2, tq, SB_KEY_TILE), F32),
            pltpu.VMEM((2, 2, tq, SB_KEY_TILE), F32),
            pltpu.VMEM((2, tq, SB_KEY_TILE), F32),
            pltpu.VMEM((2, tq, LANES), F32),
        ],
        compiler_params=_params("parallel", "parallel", "arbitrary"),
        name="stick_breaking",
    )(z, z, z)


def _rwkv_kernel(r_ref, k_ref, v_ref, lo_ref, mur_ref, muk_ref, muv_ref, mulo_ref,
                 wup_ref, w0_ref, aup_ref, a0_ref, gup_ref, kk_ref, ka_ref, rk_ref,
                 lnw_ref, lnb_ref, o_ref, s_ref, pr_ref, pk_ref, pv_ref, plo_ref, *, tc):
    C = RW_CHUNK
    C2 = 2 * C
    nchunk = tc // C
    npair = RW_PAIRS_PER_STEP
    width = npair * LANES
    t = pl.program_id(2)

    @pl.when(t == 0)
    def _():
        s_ref[...] = jnp.zeros_like(s_ref)
        pr_ref[...] = jnp.zeros_like(pr_ref)
        pk_ref[...] = jnp.zeros_like(pk_ref)
        pv_ref[...] = jnp.zeros_like(pv_ref)
        plo_ref[...] = jnp.zeros_like(plo_ref)

    def shifted(z_ref, prev_ref, mu_ref):
        z = z_ref[...]
        row = lax.broadcasted_iota(jnp.int32, z.shape, 0)
        prev = jnp.where(row == 0, prev_ref[0:1, :], pltpu.roll(z, 1, axis=0))
        prev_ref[0:1, :] = z[tc - 1:tc, :]
        return z + (prev - z) * mu_ref[...]

    r = shifted(r_ref, pr_ref, mur_ref)
    k = shifted(k_ref, pk_ref, muk_ref)
    v = shifted(v_ref, pv_ref, muv_ref)
    lo = shifted(lo_ref, plo_ref, mulo_ref)
    xw = lo[:, :DECAY_LORA]
    xa = lo[:, DECAY_LORA:DECAY_LORA + ICL_LORA]
    xg = lo[:, DECAY_LORA + ICL_LORA:]

    wl = w0_ref[...] + _dot(jnp.tanh(xw).astype(BF16), wup_ref[...])
    w_log = -(jnp.maximum(-wl, 0.0) + jnp.log1p(jnp.exp(-jnp.abs(wl)))) - 0.5
    lw = -jnp.exp(w_log)
    a = jax.nn.sigmoid(a0_ref[...] + _dot(xa.astype(BF16), aup_ref[...]))
    g = _dot(jax.nn.sigmoid(xg).astype(BF16), gup_ref[...])

    li = lax.broadcasted_iota(jnp.int32, (width, width), 0)
    lj = lax.broadcasted_iota(jnp.int32, (width, width), 1)
    head_ones = jnp.where(li // HEAD_DIM == lj // HEAD_DIM, 1.0, 0.0).astype(BF16)

    def head_sum(x):
        return _dot(x.astype(BF16), head_ones)

    kk = k * kk_ref[...]
    kk = kk / jnp.maximum(jnp.sqrt(head_sum(kk * kk)), 1e-12)
    kmod = k * (1.0 + (a - 1.0) * ka_ref[...])
    bonus = head_sum(r * kmod * rk_ref[...]) * v
    kka = kk * a

    lane = lax.broadcasted_iota(jnp.int32, (C, LANES), 1)
    head0 = lane < HEAD_DIM

    def stack(x):
        return jnp.concatenate([jnp.where(head0, x, 0.0), jnp.where(head0, 0.0, x)], axis=0)

    def unstack(x):
        return x[:C] + x[C:]

    ci = lax.broadcasted_iota(jnp.int32, (C, C2), 0)
    cj = lax.broadcasted_iota(jnp.int32, (C, C2), 1) & (C - 1)
    cum_mat = jnp.where(cj <= ci, 1.0, 0.0).astype(BF16)
    lw_hi, lw_lo = _split_bf16(lw)
    cum = jnp.concatenate(
        [_dot(cum_mat, jnp.concatenate([lw_hi[c * C:(c + 1) * C], lw_lo[c * C:(c + 1) * C]], axis=0))
         for c in range(nchunk)], axis=0)
    e_cum = jnp.exp(cum)
    e_inv = jnp.exp(-cum)
    rt_all = r * e_cum
    bt_all = kk * jnp.exp(cum - lw)
    at_all = -kka * e_inv
    kt_all = kmod * e_inv

    gi = lax.broadcasted_iota(jnp.int32, (2 * C2, 2 * C2), 0)
    gj = lax.broadcasted_iota(jnp.int32, (2 * C2, 2 * C2), 1)
    ti, tj = gi & (C - 1), gj & (C - 1)
    gram_mask = tj < ti + jnp.where(gi < C2, 0, 1)
    si = lax.broadcasted_iota(jnp.int32, (C2, C2), 0)
    sj = lax.broadcasted_iota(jnp.int32, (C2, C2), 1)
    eye = si == sj
    s_mask = (si // C) == (sj // C)

    zeros_c = jnp.zeros((C, LANES), F32)

    probs = [(c, p) for c in range(nchunk) for p in range(npair)]
    chunks = range(len(probs))
    sl = [(slice(c * C, (c + 1) * C), slice(p * LANES, (p + 1) * LANES)) for c, p in probs]
    rt = [rt_all[s] for s in sl]
    bt = [bt_all[s] for s in sl]
    at = [at_all[s] for s in sl]
    kt = [kt_all[s] for s in sl]
    vc = [v[s] for s in sl]
    p_last = [e_cum[(c + 1) * C - 1:(c + 1) * C, p * LANES:(p + 1) * LANES] for c, p in probs]

    gram = []
    for c in chunks:
        lhs = jnp.concatenate([stack(bt[c]), stack(rt[c])], axis=0).astype(BF16)
        rhs = jnp.concatenate([stack(at[c]), stack(kt[c])], axis=0).astype(BF16)
        gram.append(jnp.where(gram_mask, _dot_nt(lhs, rhs), 0.0))
    n_ab = [gm[:C2, :C2] for gm in gram]
    a_rows = [gm[C2:].astype(BF16) for gm in gram]
    sv = [stack(v_c).astype(BF16) for v_c in vc]
    x0 = [unstack(_dot(gram[c][:C2, C2:].astype(BF16), sv[c])) for c in chunks]

    tinv = [jnp.where(eye, 1.0, n) for n in n_ab]
    pw = [_dot(n.astype(BF16), n.astype(BF16)) for n in n_ab]
    for lvl in range(2, int(math.log2(C)) + 1):
        last = lvl == int(math.log2(C))
        rhs = [tinv[c].astype(BF16) if last else
               jnp.concatenate([tinv[c], pw[c]], axis=1).astype(BF16) for c in chunks]
        prod = [_dot(pw[c].astype(BF16), rhs[c]) for c in chunks]
        tinv = [tinv[c] + prod[c][:, :C2] for c in chunks]
        if not last:
            pw = [prod[c][:, C2:] for c in chunks]

    tb = [_dot(tinv[c].astype(BF16),
               jnp.concatenate([stack(bt[c]), stack(x0[c])], axis=1).astype(BF16)) for c in chunks]
    w_c = [unstack(m[:, :LANES]) for m in tb]
    u0 = [unstack(m[:, LANES:]) for m in tb]
    zeros_s = jnp.zeros((C2, LANES), BF16)
    ab = [_dot(a_rows[c], jnp.concatenate(
              [jnp.concatenate([stack(w_c[c]), stack(u0[c])], axis=1).astype(BF16),
               jnp.concatenate([zeros_s, sv[c]], axis=1)], axis=0)) for c in chunks]
    r_eff = [(rt[c] + unstack(ab[c][:, :LANES])).astype(BF16) for c in chunks]
    y0 = [unstack(ab[c][:, LANES:]) for c in chunks]
    tn = [_dot_tn(jnp.concatenate([jnp.concatenate([w_c[c], u0[c]], axis=1),
                                   jnp.concatenate([zeros_c, vc[c]], axis=1)], axis=0).astype(BF16),
                  jnp.concatenate([at[c], kt[c]], axis=0).astype(BF16)) for c in chunks]
    a_c = [(jnp.where(s_mask, m[:C2], 0.0) * p_last[c]).astype(BF16) for c, m in enumerate(tn)]
    b_c = [jnp.where(s_mask, m[C2:], 0.0) * p_last[c] for c, m in enumerate(tn)]

    s = [s_ref[p] for p in range(npair)]
    ys = []
    for i, (c, p) in enumerate(probs):
        sb = s[p].astype(BF16)
        ys.append(y0[i] + _dot_nt(r_eff[i], sb))
        s[p] = s[p] * p_last[i] + _dot(sb, a_c[i]) + b_c[i]
    for p in range(npair):
        s_ref[p] = s[p]

    y = jnp.concatenate([jnp.concatenate(ys[c * npair:(c + 1) * npair], axis=1)
                         for c in range(nchunk)], axis=0)
    mean = head_sum(y) * (1.0 / HEAD_DIM)
    yc = y - mean
    var = head_sum(yc * yc) * (1.0 / HEAD_DIM)
    yn = yc * lax.rsqrt(var + LNX_EPS) * lnw_ref[...] + lnb_ref[...]
    o_ref[...] = ((yn + bonus) * g).astype(o_ref.dtype)


def _rwkv(z, mu, w_up, w0, a_up, a0, g_up, k_k, k_a, r_k, lnx_w, lnx_b, *, batch, seq, tc):
    n = z.shape[0]
    nt = seq // tc
    width = RW_PAIRS_PER_STEP * LANES
    groups = RW_WIDTH // width
    col0 = 3 * SB_WIDTH // width
    lo_blk = (3 * SB_WIDTH + 3 * RW_WIDTH) // LORA_PAD
    vec = lambda off: pl.BlockSpec((1, width), lambda b, p, t: (0, off + p))
    return pl.pallas_call(
        functools.partial(_rwkv_kernel, tc=tc),
        grid=(batch, groups, nt),
        in_specs=[
            pl.BlockSpec((tc, width), lambda b, p, t: (b * nt + t, col0 + p)),
            pl.BlockSpec((tc, width), lambda b, p, t: (b * nt + t, col0 + groups + p)),
            pl.BlockSpec((tc, width), lambda b, p, t: (b * nt + t, col0 + 2 * groups + p)),
            pl.BlockSpec((tc, LORA_PAD), lambda b, p, t: (b * nt + t, lo_blk)),
            vec(0), vec(groups), vec(2 * groups),
            pl.BlockSpec((1, LORA_PAD), lambda b, p, t: (0, 3 * RW_WIDTH // LORA_PAD)),
            pl.BlockSpec((DECAY_LORA, width), lambda b, p, t: (0, p)),
            vec(0),
            pl.BlockSpec((ICL_LORA, width), lambda b, p, t: (0, p)),
            vec(0),
            pl.BlockSpec((LORA_PAD - DECAY_LORA - ICL_LORA, width), lambda b, p, t: (0, p)),
            vec(0), vec(0), vec(0), vec(0), vec(0),
        ],
        out_specs=pl.BlockSpec((tc, width), lambda b, p, t: (b * nt + t, p)),
        out_shape=jax.ShapeDtypeStruct((n, RW_WIDTH), BF16),
        scratch_shapes=[
            pltpu.VMEM((RW_PAIRS_PER_STEP, LANES, LANES), F32),
            pltpu.VMEM((8, width), F32), pltpu.VMEM((8, width), F32), pltpu.VMEM((8, width), F32),
            pltpu.VMEM((8, LORA_PAD), F32),
        ],
        compiler_params=_params("parallel", "parallel", "arbitrary"),
        name="rwkv7",
    )(z, z, z, z, mu, mu, mu, mu, w_up, w0, a_up, a0, g_up, k_k, k_a, r_k, lnx_w, lnx_b)


def _sgu_kernel(x_ref, u_ref, v_ref, lg_ref, lb_ref, ws_ref, bs_ref, wo_ref, o_ref, y_ref, *, tm):
    v = v_ref[...]
    mean = jnp.mean(v, axis=-1, keepdims=True)
    vc = v - mean
    var = jnp.mean(vc * vc, axis=-1, keepdims=True)
    vn = (vc * lax.rsqrt(var + LN_EPS) * lg_ref[...] + lb_ref[...]).astype(BF16)

    ti = lax.broadcasted_iota(jnp.int32, (SGU_BLOCK, SGU_BLOCK), 0) // STREAM_CHUNK
    si = lax.broadcasted_iota(jnp.int32, (SGU_BLOCK, SGU_BLOCK), 1) // STREAM_CHUNK
    causal = si <= ti
    gw = v.shape[1] // SGU_GROUPS
    for g in range(SGU_GROUPS):
        ws = jnp.where(causal, ws_ref[g], 0.0).astype(BF16)
        cols = slice(g * gw, (g + 1) * gw)
        for blk in range(tm // SGU_BLOCK):
            rows = slice(blk * SGU_BLOCK, (blk + 1) * SGU_BLOCK)
            mixed = _dot(ws, vn[rows, cols]) + bs_ref[:, g:g + 1]
            y_ref[rows, cols] = (u_ref[rows, cols] * mixed).astype(BF16)
    o_ref[...] = x_ref[...] + _dot(y_ref[...], wo_ref[...])


def _sgu(x, zc, ln_g, ln_b, w_s, b_s_t, w_out, *, tm):
    n, d = x.shape
    return pl.pallas_call(
        functools.partial(_sgu_kernel, tm=tm),
        grid=(n // tm,),
        in_specs=[
            pl.BlockSpec((tm, d), lambda i: (i, 0)),
            pl.BlockSpec((tm, d), lambda i: (i, 0)),
            pl.BlockSpec((tm, d), lambda i: (i, 1)),
            pl.BlockSpec((1, d), lambda i: (0, 0)),
            pl.BlockSpec((1, d), lambda i: (0, 0)),
            pl.BlockSpec((SGU_GROUPS, SGU_BLOCK, SGU_BLOCK), lambda i: (0, 0, 0)),
            pl.BlockSpec((SGU_BLOCK, SGU_GROUPS), lambda i: (0, 0)),
            pl.BlockSpec((d, d), lambda i: (0, 0)),
        ],
        out_specs=pl.BlockSpec((tm, d), lambda i: (i, 0)),
        out_shape=jax.ShapeDtypeStruct((n, d), F32),
        scratch_shapes=[pltpu.VMEM((tm, d), BF16)],
        compiler_params=_params("parallel"),
        name="sgu",
    )(x, zc, zc, ln_g, ln_b, w_s, b_s_t, w_out)


def _pad_cols(w, width):
    return jnp.pad(w, ((0, 0), (0, width - w.shape[1])))


def _pad_rows(w, height):
    return jnp.pad(w, ((0, height - w.shape[0]), (0, 0)))


def kernel(x, norm_mix, norm_ffn, norm_final, even_w_in, even_mu, even_w_up, even_w0, even_a_up, even_a0, even_g_up, even_k_k, even_k_a, even_r_k, even_lnx_w, even_lnx_b, even_w_out, odd_w_in, odd_ln_g, odd_ln_b, odd_w_s, odd_b_s, odd_w_out, ffn_w1, ffn_w2):
    batch, seq, d = x.shape
    depth = norm_mix.shape[0]
    xf = x.reshape(batch * seq, d)
    row = lambda a: a.reshape(1, -1)

    rw_cols = 3 * RW_WIDTH + LORA_PAD
    in_cols = 3 * SB_WIDTH + rw_cols

    for layer in range(depth):
        i = layer // 2
        if layer % 2 == 0:
            w_in = _pad_cols(even_w_in[i], in_cols).astype(BF16)
            z = _norm_matmul(xf, row(norm_mix[layer]), w_in, tm=512, tn=in_cols // 3)
            o_sb = _sb_attention(z, batch=batch, seq=seq, tq=256)
            o_rw = _rwkv(z, _pad_cols(row(even_mu[i]), rw_cols),
                         even_w_up[i].astype(BF16), row(even_w0[i]),
                         even_a_up[i].astype(BF16), row(even_a0[i]),
                         _pad_rows(even_g_up[i], LORA_PAD - DECAY_LORA - ICL_LORA).astype(BF16),
                         row(even_k_k[i]), row(even_k_a[i]),
                         row(even_r_k[i]), row(even_lnx_w[i]), row(even_lnx_b[i]),
                         batch=batch, seq=seq, tc=512)
            xf = _proj2(xf, o_sb, o_rw, even_w_out[i].astype(BF16), tm=512)
        else:
            zc = _norm_matmul(xf, row(norm_mix[layer]), odd_w_in[i].astype(BF16),
                              tm=512, tn=1024, gelu=True)
            xf = _sgu(xf, zc, row(odd_ln_g[i]), row(odd_ln_b[i]), odd_w_s[i],
                      odd_b_s[i].T, odd_w_out[i].astype(BF16), tm=256)
        xf = _ffn(xf, row(norm_ffn[layer]), ffn_w1[layer].astype(BF16), ffn_w2[layer].astype(BF16),
                  row(norm_final), tm=512, th=2048, final_norm=(layer == depth - 1))
    return xf.reshape(batch, seq, d)
```

```python
import functools
import math

import jax
import jax.numpy as jnp
from jax import lax
from jax.experimental import pallas as pl
from jax.experimental.pallas import tpu as pltpu

F32 = jnp.float32
BF16 = jnp.bfloat16

LANES = 128
HEAD_DIM = 64
RMS_EPS = 1e-6
LN_EPS = 1e-5
LNX_EPS = 64e-5
SB_WIDTH = 1024
RW_WIDTH = 1024
DECAY_LORA = 64
ICL_LORA = 64
GATE_LORA = 160
LORA_PAD = 384
SGU_GROUPS = 8
SGU_BLOCK = 128
STREAM_CHUNK = 64
RW_CHUNK = 64
RW_PAIRS_PER_STEP = 2
VMEM_LIMIT = 56 * 1024 * 1024


def _params(*sem):
    return pltpu.CompilerParams(dimension_semantics=sem, vmem_limit_bytes=VMEM_LIMIT)


def _rms(x, g):
    return x * lax.rsqrt(jnp.mean(x * x, axis=-1, keepdims=True) + RMS_EPS) * g


def _dot(a, b):
    return jnp.dot(a, b, preferred_element_type=F32)


def _dot_nt(a, b):
    return lax.dot_general(a, b, (((1,), (1,)), ((), ())), preferred_element_type=F32)


def _dot_tn(a, b):
    return lax.dot_general(a, b, (((0,), (0,)), ((), ())), preferred_element_type=F32)


def _split_bf16(x):
    hi = x.astype(BF16)
    lo = (x - hi.astype(F32)).astype(BF16)
    return hi, lo


def _norm_matmul_kernel(x_ref, g_ref, w_ref, o_ref, h_ref, *, gelu):
    @pl.when(pl.program_id(1) == 0)
    def _():
        h_ref[...] = _rms(x_ref[...], g_ref[...]).astype(BF16)

    y = _dot(h_ref[...], w_ref[...])
    if gelu:
        y = 0.5 * y * (1.0 + lax.erf(y * math.sqrt(0.5)))
    o_ref[...] = y.astype(o_ref.dtype)


def _norm_matmul(x, g, w, *, tm, tn, gelu=False, out_dtype=F32):
    n, d = x.shape
    nout = w.shape[1]
    return pl.pallas_call(
        functools.partial(_norm_matmul_kernel, gelu=gelu),
        grid=(n // tm, nout // tn),
        in_specs=[
            pl.BlockSpec((tm, d), lambda i, j: (i, 0)),
            pl.BlockSpec((1, d), lambda i, j: (0, 0)),
            pl.BlockSpec((d, tn), lambda i, j: (0, j)),
        ],
        out_specs=pl.BlockSpec((tm, tn), lambda i, j: (i, j)),
        out_shape=jax.ShapeDtypeStruct((n, nout), out_dtype),
        scratch_shapes=[pltpu.VMEM((tm, d), BF16)],
        compiler_params=_params("parallel", "arbitrary"),
        name="norm_matmul_gelu" if gelu else "norm_matmul",
    )(x, g, w)


def _ffn_kernel(x_ref, g_ref, w1_ref, w2_ref, gf_ref, o_ref, h_ref, *, final_norm):
    j = pl.program_id(1)

    @pl.when(j == 0)
    def _():
        x = x_ref[...]
        h_ref[...] = _rms(x, g_ref[...]).astype(BF16)
        o_ref[...] = x

    a = jnp.square(jnp.maximum(_dot(h_ref[...], w1_ref[...]), 0.0)).astype(BF16)
    o_ref[...] += _dot(a, w2_ref[...])

    if final_norm:
        @pl.when(j == pl.num_programs(1) - 1)
        def _():
            o_ref[...] = _rms(o_ref[...], gf_ref[...])


def _ffn(x, g, w1, w2, gf, *, tm, th, final_norm):
    n, d = x.shape
    hid = w1.shape[1]
    return pl.pallas_call(
        functools.partial(_ffn_kernel, final_norm=final_norm),
        grid=(n // tm, hid // th),
        in_specs=[
            pl.BlockSpec((tm, d), lambda i, j: (i, 0)),
            pl.BlockSpec((1, d), lambda i, j: (0, 0)),
            pl.BlockSpec((d, th), lambda i, j: (0, j)),
            pl.BlockSpec((th, d), lambda i, j: (j, 0)),
            pl.BlockSpec((1, d), lambda i, j: (0, 0)),
        ],
        out_specs=pl.BlockSpec((tm, d), lambda i, j: (i, 0)),
        out_shape=jax.ShapeDtypeStruct((n, d), F32),
        scratch_shapes=[pltpu.VMEM((tm, d), BF16)],
        compiler_params=_params("parallel", "arbitrary"),
        name="ffn_final" if final_norm else "ffn",
    )(x, g, w1, w2, gf)


def _proj2_kernel(x_ref, a_ref, b_ref, w_ref, o_ref):
    ka = a_ref.shape[1]
    o_ref[...] = (x_ref[...] + _dot(a_ref[...], w_ref[:ka, :]) + _dot(b_ref[...], w_ref[ka:, :]))


def _proj2(x, a, b, w, *, tm):
    n, d = x.shape
    ka, kb = a.shape[1], b.shape[1]
    return pl.pallas_call(
        _proj2_kernel,
        grid=(n // tm,),
        in_specs=[
            pl.BlockSpec((tm, d), lambda i: (i, 0)),
            pl.BlockSpec((tm, ka), lambda i: (i, 0)),
            pl.BlockSpec((tm, kb), lambda i: (i, 0)),
            pl.BlockSpec((ka + kb, d), lambda i: (0, 0)),
        ],
        out_specs=pl.BlockSpec((tm, d), lambda i: (i, 0)),
        out_shape=jax.ShapeDtypeStruct((n, d), F32),
        compiler_params=_params("parallel"),
        name="out_proj",
    )(x, a, b, w)


NEG_BIG = -1e30
SB_KEY_TILE = 128
DEAD_LOG2_WEIGHT = -150.5


def _sb_kernel(q_ref, k_ref, v_ref, o_ref, z_ref, s_ref, ls_ref, e_ref, rs_ref, c_ref, acc_ref, *, tq):
    tk = SB_KEY_TILE
    per_q = tq // tk
    assert per_q == 2
    qi = pl.program_id(2)
    n_tiles = (qi + 1) * per_q
    lane = lax.broadcasted_iota(jnp.int32, (tq, LANES), 1)
    head0 = lane < HEAD_DIM
    q = q_ref[...] * (HEAD_DIM ** -0.5 * math.log2(math.e))
    qh = (jnp.where(head0, q, 0.0).astype(BF16), jnp.where(head0, 0.0, q).astype(BF16))

    jj = lax.broadcasted_iota(jnp.int32, (tk, 2 * tk), 0)
    ss = lax.broadcasted_iota(jnp.int32, (tk, 2 * tk), 1)
    tail_mat = jnp.where((ss >= tk) | (jj > ss), -1.0, 0.0).astype(BF16)

    def key_tile(m):
        return jnp.maximum(n_tiles - 1 - m, 0)

    def tile_rows(m):
        return pl.ds(pl.multiple_of(key_tile(m) * tk, tk), tk)

    def stage_a1(m, slot):
        kb = k_ref[tile_rows(m), :].astype(BF16)
        for h in range(2):
            z_ref[h, slot] = _dot_nt(qh[h], kb)

    def stage_a2(slot, masked_tile=None):
        for h in range(2):
            z = z_ref[h, slot]
            sp = jnp.maximum(z, 0.0) + jnp.log2(1.0 + jnp.exp2(-jnp.abs(z)))
            log_beta = z - sp
            if masked_tile is not None:
                t_pos = qi * tq + lax.broadcasted_iota(jnp.int32, (tq, tk), 0)
                s_pos = key_tile(masked_tile) * tk + lax.broadcasted_iota(jnp.int32, (tq, tk), 1)
                sp = jnp.where(s_pos < t_pos, sp, 0.0)
                log_beta = jnp.where(s_pos < t_pos, log_beta, NEG_BIG)
            s_ref[h, slot] = sp.astype(BF16)
            ls_ref[h, slot] = log_beta

    def stage_b(slot):
        for h in range(2):
            both = _dot(s_ref[h, slot], tail_mat)
            e_ref[h, slot] = ls_ref[h, slot] + both[:, :tk]
            rs_ref[h, slot] = both[:, tk:]

    def stage_c(m, slot):
        vb = v_ref[tile_rows(m), :].astype(BF16)
        for h in range(2):
            c = c_ref[h]
            w = jnp.exp2(e_ref[h, slot] + c)
            acc_ref[h] += _dot(w.astype(BF16), vb)
            c_ref[h] = c + rs_ref[h, slot]

    c_ref[...] = jnp.zeros_like(c_ref)
    acc_ref[...] = jnp.zeros_like(acc_ref)

    stage_a1(0, 0)
    stage_a2(0, masked_tile=0)
    stage_a1(1, 1)
    stage_b(0)
    stage_a2(1, masked_tile=1 if per_q > 1 else None)
    stage_a1(2, 0)

    def two_steps(state):
        pair, _ = state
        i = 2 * pair + 3
        for s_new, s_old in ((1, 0), (0, 1)):
            stage_a1(i, s_new)
            stage_b(s_new)
            stage_c(i - 3, s_old)
            stage_a2(s_old)
            i = i + 1
        return pair + 1, jnp.max(c_ref[...]) >= DEAD_LOG2_WEIGHT

    first = two_steps((0, None))
    lax.while_loop(lambda state: (state[0] < n_tiles // 2) & state[1], two_steps,
                   (jnp.int32(first[0]), first[1]))

    o_ref[...] = jnp.where(head0, acc_ref[0], acc_ref[1]).astype(o_ref.dtype)


def _sb_attention(z, *, batch, seq, tq):
    n = z.shape[0]
    nq = seq // tq
    pairs = SB_WIDTH // LANES
    return pl.pallas_call(
        functools.partial(_sb_kernel, tq=tq),
        grid=(batch, pairs, nq),
        in_specs=[
            pl.BlockSpec((tq, LANES), lambda b, p, i: (b * nq + i, p)),
            pl.BlockSpec((seq, LANES), lambda b, p, i: (b, pairs + p)),
            pl.BlockSpec((seq, LANES), lambda b, p, i: (b, 2 * pairs + p)),
        ],
        out_specs=pl.BlockSpec((tq, LANES), lambda b, p, i: (b * nq + i, p)),
        out_shape=jax.ShapeDtypeStruct((n, SB_WIDTH), BF16),
        scratch_shapes=[
            pltpu.VMEM((2, 2, tq, SB_KEY_TILE), F32),
            pltpu.VMEM((2, 2, tq, SB_KEY_TILE), BF16),
            pltpu.VMEM((2, 2, tq, SB_KEY_TILE), F32),
            pltpu.VMEM((2, 2, tq, SB_KEY_TILE), F32),
            pltpu.VMEM((2, 2, tq, SB_KEY_TILE), F32),
            pltpu.VMEM((2, tq, SB_KEY_TILE), F32),
            pltpu.VMEM((2, tq, LANES), F32),
        ],
        compiler_params=_params("parallel", "parallel", "arbitrary"),
        name="stick_breaking",
    )(z, z, z)


def _rwkv_kernel(r_ref, k_ref, v_ref, lo_ref, mur_ref, muk_ref, muv_ref, mulo_ref,
                 wup_ref, w0_ref, aup_ref, a0_ref, gup_ref, kk_ref, ka_ref, rk_ref,
                 lnw_ref, lnb_ref, o_ref, s_ref, pr_ref, pk_ref, pv_ref, plo_ref, *, tc):
    C = RW_CHUNK
    C2 = 2 * C
    nchunk = tc // C
    npair = RW_PAIRS_PER_STEP
    width = npair * LANES
    t = pl.program_id(2)

    @pl.when(t == 0)
    def _():
        s_ref[...] = jnp.zeros_like(s_ref)
        pr_ref[...] = jnp.zeros_like(pr_ref)
        pk_ref[...] = jnp.zeros_like(pk_ref)
        pv_ref[...] = jnp.zeros_like(pv_ref)
        plo_ref[...] = jnp.zeros_like(plo_ref)

    def shifted(z_ref, prev_ref, mu_ref):
        z = z_ref[...]
        row = lax.broadcasted_iota(jnp.int32, z.shape, 0)
        prev = jnp.where(row == 0, prev_ref[0:1, :], pltpu.roll(z, 1, axis=0))
        prev_ref[0:1, :] = z[tc - 1:tc, :]
        return z + (prev - z) * mu_ref[...]

    r = shifted(r_ref, pr_ref, mur_ref)
    k = shifted(k_ref, pk_ref, muk_ref)
    v = shifted(v_ref, pv_ref, muv_ref)
    lo = shifted(lo_ref, plo_ref, mulo_ref)
    xw = lo[:, :DECAY_LORA]
    xa = lo[:, DECAY_LORA:DECAY_LORA + ICL_LORA]
    xg = lo[:, DECAY_LORA + ICL_LORA:]

    wl = w0_ref[...] + _dot(jnp.tanh(xw).astype(BF16), wup_ref[...])
    w_log = -(jnp.maximum(-wl, 0.0) + jnp.log1p(jnp.exp(-jnp.abs(wl)))) - 0.5
    lw = -jnp.exp(w_log)
    a = jax.nn.sigmoid(a0_ref[...] + _dot(xa.astype(BF16), aup_ref[...]))
    g = _dot(jax.nn.sigmoid(xg).astype(BF16), gup_ref[...])

    li = lax.broadcasted_iota(jnp.int32, (width, width), 0)
    lj = lax.broadcasted_iota(jnp.int32, (width, width), 1)
    head_ones = jnp.where(li // HEAD_DIM == lj // HEAD_DIM, 1.0, 0.0).astype(BF16)

    def head_sum(x):
        return _dot(x.astype(BF16), head_ones)

    kk = k * kk_ref[...]
    kk = kk / jnp.maximum(jnp.sqrt(head_sum(kk * kk)), 1e-12)
    kmod = k * (1.0 + (a - 1.0) * ka_ref[...])
    bonus = head_sum(r * kmod * rk_ref[...]) * v
    kka = kk * a

    lane = lax.broadcasted_iota(jnp.int32, (C, LANES), 1)
    head0 = lane < HEAD_DIM

    def stack(x):
        return jnp.concatenate([jnp.where(head0, x, 0.0), jnp.where(head0, 0.0, x)], axis=0)

    def unstack(x):
        return x[:C] + x[C:]

    ci = lax.broadcasted_iota(jnp.int32, (C, C2), 0)
    cj = lax.broadcasted_iota(jnp.int32, (C, C2), 1) & (C - 1)
    cum_mat = jnp.where(cj <= ci, 1.0, 0.0).astype(BF16)
    lw_hi, lw_lo = _split_bf16(lw)
    cum = jnp.concatenate(
        [_dot(cum_mat, jnp.concatenate([lw_hi[c * C:(c + 1) * C], lw_lo[c * C:(c + 1) * C]], axis=0))
         for c in range(nchunk)], axis=0)
    e_cum = jnp.exp(cum)
    e_inv = jnp.exp(-cum)
    rt_all = r * e_cum
    bt_all = kk * jnp.exp(cum - lw)
    at_all = -kka * e_inv
    kt_all = kmod * e_inv

    gi = lax.broadcasted_iota(jnp.int32, (2 * C2, 2 * C2), 0)
    gj = lax.broadcasted_iota(jnp.int32, (2 * C2, 2 * C2), 1)
    ti, tj = gi & (C - 1), gj & (C - 1)
    gram_mask = tj < ti + jnp.where(gi < C2, 0, 1)
    si = lax.broadcasted_iota(jnp.int32, (C2, C2), 0)
    sj = lax.broadcasted_iota(jnp.int32, (C2, C2), 1)
    eye = si == sj
    s_mask = (si // C) == (sj // C)

    zeros_c = jnp.zeros((C, LANES), F32)

    probs = [(c, p) for c in range(nchunk) for p in range(npair)]
    chunks = range(len(probs))
    sl = [(slice(c * C, (c + 1) * C), slice(p * LANES, (p + 1) * LANES)) for c, p in probs]
    rt = [rt_all[s] for s in sl]
    bt = [bt_all[s] for s in sl]
    at = [at_all[s] for s in sl]
    kt = [kt_all[s] for s in sl]
    vc = [v[s] for s in sl]
    p_last = [e_cum[(c + 1) * C - 1:(c + 1) * C, p * LANES:(p + 1) * LANES] for c, p in probs]

    gram = []
    for c in chunks:
        lhs = jnp.concatenate([stack(bt[c]), stack(rt[c])], axis=0).astype(BF16)
        rhs = jnp.concatenate([stack(at[c]), stack(kt[c])], axis=0).astype(BF16)
        gram.append(jnp.where(gram_mask, _dot_nt(lhs, rhs), 0.0))
    n_ab = [gm[:C2, :C2] for gm in gram]
    a_rows = [gm[C2:].astype(BF16) for gm in gram]
    sv = [stack(v_c).astype(BF16) for v_c in vc]
    x0 = [unstack(_dot(gram[c][:C2, C2:].astype(BF16), sv[c])) for c in chunks]

    tinv = [jnp.where(eye, 1.0, n) for n in n_ab]
    pw = [_dot(n.astype(BF16), n.astype(BF16)) for n in n_ab]
    for lvl in range(2, int(math.log2(C)) + 1):
        last = lvl == int(math.log2(C))
        rhs = [tinv[c].astype(BF16) if last else
               jnp.concatenate([tinv[c], pw[c]], axis=1).astype(BF16) for c in chunks]
        prod = [_dot(pw[c].astype(BF16), rhs[c]) for c in chunks]
        tinv = [tinv[c] + prod[c][:, :C2] for c in chunks]
        if not last:
            pw = [prod[c][:, C2:] for c in chunks]

    tb = [_dot(tinv[c].astype(BF16),
               jnp.concatenate([stack(bt[c]), stack(x0[c])], axis=1).astype(BF16)) for c in chunks]
    w_c = [unstack(m[:, :LANES]) for m in tb]
    u0 = [unstack(m[:, LANES:]) for m in tb]
    zeros_s = jnp.zeros((C2, LANES), BF16)
    ab = [_dot(a_rows[c], jnp.concatenate(
              [jnp.concatenate([stack(w_c[c]), stack(u0[c])], axis=1).astype(BF16),
               jnp.concatenate([zeros_s, sv[c]], axis=1)], axis=0)) for c in chunks]
    r_eff = [(rt[c] + unstack(ab[c][:, :LANES])).astype(BF16) for c in chunks]
    y0 = [unstack(ab[c][:, LANES:]) for c in chunks]
    tn = [_dot_tn(jnp.concatenate([jnp.concatenate([w_c[c], u0[c]], axis=1),
                                   jnp.concatenate([zeros_c, vc[c]], axis=1)], axis=0).astype(BF16),
                  jnp.concatenate([at[c], kt[c]], axis=0).astype(BF16)) for c in chunks]
    a_c = [(jnp.where(s_mask, m[:C2], 0.0) * p_last[c]).astype(BF16) for c, m in enumerate(tn)]
    b_c = [jnp.where(s_mask, m[C2:], 0.0) * p_last[c] for c, m in enumerate(tn)]

    s = [s_ref[p] for p in range(npair)]
    ys = []
    for i, (c, p) in enumerate(probs):
        sb = s[p].astype(BF16)
        ys.append(y0[i] + _dot_nt(r_eff[i], sb))
        s[p] = s[p] * p_last[i] + _dot(sb, a_c[i]) + b_c[i]
    for p in range(npair):
        s_ref[p] = s[p]

    y = jnp.concatenate([jnp.concatenate(ys[c * npair:(c + 1) * npair], axis=1)
                         for c in range(nchunk)], axis=0)
    mean = head_sum(y) * (1.0 / HEAD_DIM)
    yc = y - mean
    var = head_sum(yc * yc) * (1.0 / HEAD_DIM)
    yn = yc * lax.rsqrt(var + LNX_EPS) * lnw_ref[...] + lnb_ref[...]
    o_ref[...] = ((yn + bonus) * g).astype(o_ref.dtype)


def _rwkv(z, mu, w_up, w0, a_up, a0, g_up, k_k, k_a, r_k, lnx_w, lnx_b, *, batch, seq, tc):
    n = z.shape[0]
    nt = seq // tc
    width = RW_PAIRS_PER_STEP * LANES
    groups = RW_WIDTH // width
    col0 = 3 * SB_WIDTH // width
    lo_blk = (3 * SB_WIDTH + 3 * RW_WIDTH) // LORA_PAD
    vec = lambda off: pl.BlockSpec((1, width), lambda b, p, t: (0, off + p))
    return pl.pallas_call(
        functools.partial(_rwkv_kernel, tc=tc),
        grid=(batch, groups, nt),
        in_specs=[
            pl.BlockSpec((tc, width), lambda b, p, t: (b * nt + t, col0 + p)),
            pl.BlockSpec((tc, width), lambda b, p, t: (b * nt + t, col0 + groups + p)),
            pl.BlockSpec((tc, width), lambda b, p, t: (b * nt + t, col0 + 2 * groups + p)),
            pl.BlockSpec((tc, LORA_PAD), lambda b, p, t: (b * nt + t, lo_blk)),
            vec(0), vec(groups), vec(2 * groups),
            pl.BlockSpec((1, LORA_PAD), lambda b, p, t: (0, 3 * RW_WIDTH // LORA_PAD)),
            pl.BlockSpec((DECAY_LORA, width), lambda b, p, t: (0, p)),
            vec(0),
            pl.BlockSpec((ICL_LORA, width), lambda b, p, t: (0, p)),
            vec(0),
            pl.BlockSpec((LORA_PAD - DECAY_LORA - ICL_LORA, width), lambda b, p, t: (0, p)),
            vec(0), vec(0), vec(0), vec(0), vec(0),
        ],
        out_specs=pl.BlockSpec((tc, width), lambda b, p, t: (b * nt + t, p)),
        out_shape=jax.ShapeDtypeStruct((n, RW_WIDTH), BF16),
        scratch_shapes=[
            pltpu.VMEM((RW_PAIRS_PER_STEP, LANES, LANES), F32),
            pltpu.VMEM((8, width), F32), pltpu.VMEM((8, width), F32), pltpu.VMEM((8, width), F32),
            pltpu.VMEM((8, LORA_PAD), F32),
        ],
        compiler_params=_params("parallel", "parallel", "arbitrary"),
        name="rwkv7",
    )(z, z, z, z, mu, mu, mu, mu, w_up, w0, a_up, a0, g_up, k_k, k_a, r_k, lnx_w, lnx_b)


def _sgu_kernel(x_ref, u_ref, v_ref, lg_ref, lb_ref, ws_ref, bs_ref, wo_ref, o_ref, y_ref, *, tm):
    v = v_ref[...]
    mean = jnp.mean(v, axis=-1, keepdims=True)
    vc = v - mean
    var = jnp.mean(vc * vc, axis=-1, keepdims=True)
    vn = (vc * lax.rsqrt(var + LN_EPS) * lg_ref[...] + lb_ref[...]).astype(BF16)

    ti = lax.broadcasted_iota(jnp.int32, (SGU_BLOCK, SGU_BLOCK), 0) // STREAM_CHUNK
    si = lax.broadcasted_iota(jnp.int32, (SGU_BLOCK, SGU_BLOCK), 1) // STREAM_CHUNK
    causal = si <= ti
    gw = v.shape[1] // SGU_GROUPS
    for g in range(SGU_GROUPS):
        ws = jnp.where(causal, ws_ref[g], 0.0).astype(BF16)
        cols = slice(g * gw, (g + 1) * gw)
        for blk in range(tm // SGU_BLOCK):
            rows = slice(blk * SGU_BLOCK, (blk + 1) * SGU_BLOCK)
            mixed = _dot(ws, vn[rows, cols]) + bs_ref[:, g:g + 1]
            y_ref[rows, cols] = (u_ref[rows, cols] * mixed).astype(BF16)
    o_ref[...] = x_ref[...] + _dot(y_ref[...], wo_ref[...])


def _sgu(x, zc, ln_g, ln_b, w_s, b_s_t, w_out, *, tm):
    n, d = x.shape
    return pl.pallas_call(
        functools.partial(_sgu_kernel, tm=tm),
        grid=(n // tm,),
        in_specs=[
            pl.BlockSpec((tm, d), lambda i: (i, 0)),
            pl.BlockSpec((tm, d), lambda i: (i, 0)),
            pl.BlockSpec((tm, d), lambda i: (i, 1)),
            pl.BlockSpec((1, d), lambda i: (0, 0)),
            pl.BlockSpec((1, d), lambda i: (0, 0)),
            pl.BlockSpec((SGU_GROUPS, SGU_BLOCK, SGU_BLOCK), lambda i: (0, 0, 0)),
            pl.BlockSpec((SGU_BLOCK, SGU_GROUPS), lambda i: (0, 0)),
            pl.BlockSpec((d, d), lambda i: (0, 0)),
        ],
        out_specs=pl.BlockSpec((tm, d), lambda i: (i, 0)),
        out_shape=jax.ShapeDtypeStruct((n, d), F32),
        scratch_shapes=[pltpu.VMEM((tm, d), BF16)],
        compiler_params=_params("parallel"),
        name="sgu",
    )(x, zc, zc, ln_g, ln_b, w_s, b_s_t, w_out)


def _pad_cols(w, width):
    return jnp.pad(w, ((0, 0), (0, width - w.shape[1])))


def _pad_rows(w, height):
    return jnp.pad(w, ((0, height - w.shape[0]), (0, 0)))


def kernel(x, norm_mix, norm_ffn, norm_final, even_w_in, even_mu, even_w_up, even_w0, even_a_up, even_a0, even_g_up, even_k_k, even_k_a, even_r_k, even_lnx_w, even_lnx_b, even_w_out, odd_w_in, odd_ln_g, odd_ln_b, odd_w_s, odd_b_s, odd_w_out, ffn_w1, ffn_w2):
    batch, seq, d = x.shape
    depth = norm_mix.shape[0]
    xf = x.reshape(batch * seq, d)
    row = lambda a: a.reshape(1, -1)

    rw_cols = 3 * RW_WIDTH + LORA_PAD
    in_cols = 3 * SB_WIDTH + rw_cols + LANES

    for layer in range(depth):
        i = layer // 2
        if layer % 2 == 0:
            w_in = _pad_cols(even_w_in[i], in_cols).astype(BF16)
            z = _norm_matmul(xf, row(norm_mix[layer]), w_in, tm=512, tn=in_cols // 2)
            o_sb = _sb_attention(z, batch=batch, seq=seq, tq=256)
            o_rw = _rwkv(z, _pad_cols(row(even_mu[i]), rw_cols),
                         even_w_up[i].astype(BF16), row(even_w0[i]),
                         even_a_up[i].astype(BF16), row(even_a0[i]),
                         _pad_rows(even_g_up[i], LORA_PAD - DECAY_LORA - ICL_LORA).astype(BF16),
                         row(even_k_k[i]), row(even_k_a[i]),
                         row(even_r_k[i]), row(even_lnx_w[i]), row(even_lnx_b[i]),
                         batch=batch, seq=seq, tc=512)
            xf = _proj2(xf, o_sb, o_rw, even_w_out[i].astype(BF16), tm=512)
        else:
            zc = _norm_matmul(xf, row(norm_mix[layer]), odd_w_in[i].astype(BF16),
                              tm=512, tn=2048, gelu=True)
            xf = _sgu(xf, zc, row(odd_ln_g[i]), row(odd_ln_b[i]), odd_w_s[i],
                      odd_b_s[i].T, odd_w_out[i].astype(BF16), tm=256)
        xf = _ffn(xf, row(norm_ffn[layer]), ffn_w1[layer].astype(BF16), ffn_w2[layer].astype(BF16),
                  row(norm_final), tm=512, th=2048, final_norm=(layer == depth - 1))
    return xf.reshape(batch, seq, d)
```

```python
import functools
import math

import jax
import jax.numpy as jnp
from jax import lax
from jax.experimental import pallas as pl
from jax.experimental.pallas import tpu as pltpu

F32 = jnp.float32
BF16 = jnp.bfloat16

LANES = 128
HEAD_DIM = 64
RMS_EPS = 1e-6
LN_EPS = 1e-5
LNX_EPS = 64e-5
SB_WIDTH = 1024
RW_WIDTH = 1024
DECAY_LORA = 64
ICL_LORA = 64
LORA_PAD = 384
SGU_GROUPS = 8
SGU_BLOCK = 128
STREAM_CHUNK = 64
RW_CHUNK = 64
RW_PAIRS_PER_STEP = 2

VMEM_LIMIT = 56 * 1024 * 1024
ROW_TILE = 512
FFN_HIDDEN_TILE = 2048
IN_PROJ_COL_STEPS = 2
GELU_COL_TILE = 2048
SGU_ROW_TILE = 256
SB_QUERY_TILE = 256
RW_TIME_TILE = 512


def _params(*sem):
    return pltpu.CompilerParams(dimension_semantics=sem, vmem_limit_bytes=VMEM_LIMIT)


def _rms(x, g):
    return x * lax.rsqrt(jnp.mean(x * x, axis=-1, keepdims=True) + RMS_EPS) * g


def _dot(a, b):
    return jnp.dot(a, b, preferred_element_type=F32)


def _dot_nt(a, b):
    return lax.dot_general(a, b, (((1,), (1,)), ((), ())), preferred_element_type=F32)


def _dot_tn(a, b):
    return lax.dot_general(a, b, (((0,), (0,)), ((), ())), preferred_element_type=F32)


def _split_bf16(x):
    hi = x.astype(BF16)
    lo = (x - hi.astype(F32)).astype(BF16)
    return hi, lo


def _norm_matmul_kernel(x_ref, g_ref, w_ref, o_ref, h_ref, *, gelu):
    @pl.when(pl.program_id(1) == 0)
    def _():
        h_ref[...] = _rms(x_ref[...], g_ref[...]).astype(BF16)

    y = _dot(h_ref[...], w_ref[...])
    if gelu:
        y = 0.5 * y * (1.0 + lax.erf(y * math.sqrt(0.5)))
    o_ref[...] = y.astype(o_ref.dtype)


def _norm_matmul(x, g, w, *, tm, tn, gelu=False, out_dtype=F32):
    n, d = x.shape
    nout = w.shape[1]
    return pl.pallas_call(
        functools.partial(_norm_matmul_kernel, gelu=gelu),
        grid=(n // tm, nout // tn),
        in_specs=[
            pl.BlockSpec((tm, d), lambda i, j: (i, 0)),
            pl.BlockSpec((1, d), lambda i, j: (0, 0)),
            pl.BlockSpec((d, tn), lambda i, j: (0, j)),
        ],
        out_specs=pl.BlockSpec((tm, tn), lambda i, j: (i, j)),
        out_shape=jax.ShapeDtypeStruct((n, nout), out_dtype),
        scratch_shapes=[pltpu.VMEM((tm, d), BF16)],
        compiler_params=_params("parallel", "arbitrary"),
        name="norm_matmul_gelu" if gelu else "norm_matmul",
    )(x, g, w)


def _ffn_kernel(x_ref, g_ref, w1_ref, w2_ref, gf_ref, o_ref, h_ref, *, final_norm):
    j = pl.program_id(1)

    @pl.when(j == 0)
    def _():
        x = x_ref[...]
        h_ref[...] = _rms(x, g_ref[...]).astype(BF16)
        o_ref[...] = x

    a = jnp.square(jnp.maximum(_dot(h_ref[...], w1_ref[...]), 0.0)).astype(BF16)
    o_ref[...] += _dot(a, w2_ref[...])

    if final_norm:
        @pl.when(j == pl.num_programs(1) - 1)
        def _():
            o_ref[...] = _rms(o_ref[...], gf_ref[...])


def _ffn(x, g, w1, w2, gf, *, tm, th, final_norm):
    n, d = x.shape
    hid = w1.shape[1]
    return pl.pallas_call(
        functools.partial(_ffn_kernel, final_norm=final_norm),
        grid=(n // tm, hid // th),
        in_specs=[
            pl.BlockSpec((tm, d), lambda i, j: (i, 0)),
            pl.BlockSpec((1, d), lambda i, j: (0, 0)),
            pl.BlockSpec((d, th), lambda i, j: (0, j)),
            pl.BlockSpec((th, d), lambda i, j: (j, 0)),
            pl.BlockSpec((1, d), lambda i, j: (0, 0)),
        ],
        out_specs=pl.BlockSpec((tm, d), lambda i, j: (i, 0)),
        out_shape=jax.ShapeDtypeStruct((n, d), F32),
        scratch_shapes=[pltpu.VMEM((tm, d), BF16)],
        compiler_params=_params("parallel", "arbitrary"),
        name="ffn_final" if final_norm else "ffn",
    )(x, g, w1, w2, gf)


def _proj2_kernel(x_ref, a_ref, b_ref, w_ref, o_ref):
    ka = a_ref.shape[1]
    o_ref[...] = (x_ref[...] + _dot(a_ref[...], w_ref[:ka, :]) + _dot(b_ref[...], w_ref[ka:, :]))


def _proj2(x, a, b, w, *, tm):
    n, d = x.shape
    ka, kb = a.shape[1], b.shape[1]
    return pl.pallas_call(
        _proj2_kernel,
        grid=(n // tm,),
        in_specs=[
            pl.BlockSpec((tm, d), lambda i: (i, 0)),
            pl.BlockSpec((tm, ka), lambda i: (i, 0)),
            pl.BlockSpec((tm, kb), lambda i: (i, 0)),
            pl.BlockSpec((ka + kb, d), lambda i: (0, 0)),
        ],
        out_specs=pl.BlockSpec((tm, d), lambda i: (i, 0)),
        out_shape=jax.ShapeDtypeStruct((n, d), F32),
        compiler_params=_params("parallel"),
        name="out_proj",
    )(x, a, b, w)


NEG_BIG = -1e30
SB_KEY_TILE = 128
DEAD_LOG2_WEIGHT = -150.5


def _sb_kernel(q_ref, k_ref, v_ref, o_ref, z_ref, s_ref, ls_ref, e_ref, rs_ref, c_ref, acc_ref, *, tq):
    tk = SB_KEY_TILE
    per_q = tq // tk
    assert per_q == 2
    qi = pl.program_id(2)
    n_tiles = (qi + 1) * per_q
    lane = lax.broadcasted_iota(jnp.int32, (tq, LANES), 1)
    head0 = lane < HEAD_DIM
    q = q_ref[...] * (HEAD_DIM ** -0.5 * math.log2(math.e))
    qh = (jnp.where(head0, q, 0.0).astype(BF16), jnp.where(head0, 0.0, q).astype(BF16))

    jj = lax.broadcasted_iota(jnp.int32, (tk, 2 * tk), 0)
    ss = lax.broadcasted_iota(jnp.int32, (tk, 2 * tk), 1)
    tail_mat = jnp.where((ss >= tk) | (jj > ss), -1.0, 0.0).astype(BF16)

    def key_tile(m):
        return jnp.maximum(n_tiles - 1 - m, 0)

    def tile_rows(m):
        return pl.ds(pl.multiple_of(key_tile(m) * tk, tk), tk)

    def stage_a1(m, slot):
        kb = k_ref[tile_rows(m), :].astype(BF16)
        for h in range(2):
            z_ref[h, slot] = _dot_nt(qh[h], kb)

    def stage_a2(slot, masked_tile=None, guard_tile=None):
        for h in range(2):
            z = z_ref[h, slot]
            sp = jnp.maximum(z, 0.0) + jnp.log2(1.0 + jnp.exp2(-jnp.abs(z)))
            log_beta = z - sp
            if guard_tile is not None:
                sp = jnp.where(guard_tile < n_tiles, sp, 0.0)
                log_beta = jnp.where(guard_tile < n_tiles, log_beta, NEG_BIG)
            if masked_tile is not None:
                t_pos = qi * tq + lax.broadcasted_iota(jnp.int32, (tq, tk), 0)
                s_pos = key_tile(masked_tile) * tk + lax.broadcasted_iota(jnp.int32, (tq, tk), 1)
                sp = jnp.where(s_pos < t_pos, sp, 0.0)
                log_beta = jnp.where(s_pos < t_pos, log_beta, NEG_BIG)
            s_ref[h, slot] = sp.astype(BF16)
            ls_ref[h, slot] = log_beta

    def stage_b(slot):
        for h in range(2):
            both = _dot(s_ref[h, slot], tail_mat)
            e_ref[h, slot] = ls_ref[h, slot] + both[:, :tk]
            rs_ref[h, slot] = both[:, tk:]

    def stage_c(m, slot):
        vb = v_ref[tile_rows(m), :].astype(BF16)
        for h in range(2):
            c = c_ref[h]
            w = jnp.exp2(e_ref[h, slot] + c)
            acc_ref[h] += _dot(w.astype(BF16), vb)
            c_ref[h] = c + rs_ref[h, slot]

    c_ref[...] = jnp.zeros_like(c_ref)
    acc_ref[...] = jnp.zeros_like(acc_ref)

    stage_a1(0, 0)
    stage_a2(0, masked_tile=0)
    stage_a1(1, 1)
    stage_b(0)
    stage_a2(1, masked_tile=1 if per_q > 1 else None)
    stage_a1(2, 0)

    def two_steps(state, guarded=False):
        pair, _ = state
        i = 2 * pair + 3
        for s_new, s_old in ((1, 0), (0, 1)):
            stage_a1(i, s_new)
            stage_b(s_new)
            stage_c(i - 3, s_old)
            stage_a2(s_old, guard_tile=i - 1 if guarded else None)
            i = i + 1
        return pair + 1, jnp.max(c_ref[...]) >= DEAD_LOG2_WEIGHT

    state = two_steps((0, None), guarded=True)
    state = two_steps(state, guarded=True)
    lax.while_loop(lambda state: (state[0] < n_tiles // 2) & state[1], two_steps,
                   (jnp.int32(state[0]), state[1]))

    o_ref[...] = jnp.where(head0, acc_ref[0], acc_ref[1]).astype(o_ref.dtype)


def _sb_attention(z, *, batch, seq, tq):
    n = z.shape[0]
    nq = seq // tq
    pairs = SB_WIDTH // LANES
    return pl.pallas_call(
        functools.partial(_sb_kernel, tq=tq),
        grid=(batch, pairs, nq),
        in_specs=[
            pl.BlockSpec((tq, LANES), lambda b, p, i: (b * nq + i, p)),
            pl.BlockSpec((seq, LANES), lambda b, p, i: (b, pairs + p)),
            pl.BlockSpec((seq, LANES), lambda b, p, i: (b, 2 * pairs + p)),
        ],
        out_specs=pl.BlockSpec((tq, LANES), lambda b, p, i: (b * nq + i, p)),
        out_shape=jax.ShapeDtypeStruct((n, SB_WIDTH), BF16),
        scratch_shapes=[
            pltpu.VMEM((2, 2, tq, SB_KEY_TILE), F32),
            pltpu.VMEM((2, 2, tq, SB_KEY_TILE), BF16),
            pltpu.VMEM((2, 2, tq, SB_KEY_TILE), F32),
            pltpu.VMEM((2, 2, tq, SB_KEY_TILE), F32),
            pltpu.VMEM((2, 2, tq, SB_KEY_TILE), F32),
            pltpu.VMEM((2, tq, SB_KEY_TILE), F32),
            pltpu.VMEM((2, tq, LANES), F32),
        ],
        compiler_params=_params("parallel", "parallel", "arbitrary"),
        name="stick_breaking",
    )(z, z, z)


def _rwkv_kernel(r_ref, k_ref, v_ref, lo_ref, mur_ref, muk_ref, muv_ref, mulo_ref,
                 wup_ref, w0_ref, aup_ref, a0_ref, gup_ref, kk_ref, ka_ref, rk_ref,
                 lnw_ref, lnb_ref, o_ref, s_ref, pr_ref, pk_ref, pv_ref, plo_ref, *, tc):
    C = RW_CHUNK
    C2 = 2 * C
    nchunk = tc // C
    npair = RW_PAIRS_PER_STEP
    width = npair * LANES
    t = pl.program_id(2)

    @pl.when(t == 0)
    def _():
        s_ref[...] = jnp.zeros_like(s_ref)
        pr_ref[...] = jnp.zeros_like(pr_ref)
        pk_ref[...] = jnp.zeros_like(pk_ref)
        pv_ref[...] = jnp.zeros_like(pv_ref)
        plo_ref[...] = jnp.zeros_like(plo_ref)

    def shifted(z_ref, prev_ref, mu_ref):
        z = z_ref[...]
        row = lax.broadcasted_iota(jnp.int32, z.shape, 0)
        prev = jnp.where(row == 0, prev_ref[0:1, :], pltpu.roll(z, 1, axis=0))
        prev_ref[0:1, :] = z[tc - 1:tc, :]
        return z + (prev - z) * mu_ref[...]

    r = shifted(r_ref, pr_ref, mur_ref)
    k = shifted(k_ref, pk_ref, muk_ref)
    v = shifted(v_ref, pv_ref, muv_ref)
    lo = shifted(lo_ref, plo_ref, mulo_ref)
    xw = lo[:, :DECAY_LORA]
    xa = lo[:, DECAY_LORA:DECAY_LORA + ICL_LORA]
    xg = lo[:, DECAY_LORA + ICL_LORA:]

    wl = w0_ref[...] + _dot(jnp.tanh(xw).astype(BF16), wup_ref[...])
    w_log = -(jnp.maximum(-wl, 0.0) + jnp.log1p(jnp.exp(-jnp.abs(wl)))) - 0.5
    lw = -jnp.exp(w_log)
    a = jax.nn.sigmoid(a0_ref[...] + _dot(xa.astype(BF16), aup_ref[...]))
    g = _dot(jax.nn.sigmoid(xg).astype(BF16), gup_ref[...])

    li = lax.broadcasted_iota(jnp.int32, (width, width), 0)
    lj = lax.broadcasted_iota(jnp.int32, (width, width), 1)
    head_ones = jnp.where(li // HEAD_DIM == lj // HEAD_DIM, 1.0, 0.0).astype(BF16)

    def head_sum(x):
        return _dot(x.astype(BF16), head_ones)

    kk = k * kk_ref[...]
    kk = kk / jnp.maximum(jnp.sqrt(head_sum(kk * kk)), 1e-12)
    kmod = k * (1.0 + (a - 1.0) * ka_ref[...])
    bonus = head_sum(r * kmod * rk_ref[...]) * v
    kka = kk * a

    lane = lax.broadcasted_iota(jnp.int32, (C, LANES), 1)
    head0 = lane < HEAD_DIM

    def stack(x):
        return jnp.concatenate([jnp.where(head0, x, 0.0), jnp.where(head0, 0.0, x)], axis=0)

    def unstack(x):
        return x[:C] + x[C:]

    ci = lax.broadcasted_iota(jnp.int32, (C, C2), 0)
    cj = lax.broadcasted_iota(jnp.int32, (C, C2), 1) & (C - 1)
    cum_mat = jnp.where(cj <= ci, 1.0, 0.0).astype(BF16)
    lw_hi, lw_lo = _split_bf16(lw)
    cum = jnp.concatenate(
        [_dot(cum_mat, jnp.concatenate([lw_hi[c * C:(c + 1) * C], lw_lo[c * C:(c + 1) * C]], axis=0))
         for c in range(nchunk)], axis=0)
    e_cum = jnp.exp(cum)
    e_inv = jnp.exp(-cum)
    rt_all = r * e_cum
    bt_all = kk * jnp.exp(cum - lw)
    at_all = -kka * e_inv
    kt_all = kmod * e_inv

    gi = lax.broadcasted_iota(jnp.int32, (2 * C2, 2 * C2), 0)
    gj = lax.broadcasted_iota(jnp.int32, (2 * C2, 2 * C2), 1)
    ti, tj = gi & (C - 1), gj & (C - 1)
    gram_mask = tj < ti + jnp.where(gi < C2, 0, 1)
    si = lax.broadcasted_iota(jnp.int32, (C2, C2), 0)
    sj = lax.broadcasted_iota(jnp.int32, (C2, C2), 1)
    eye = si == sj
    s_mask = (si // C) == (sj // C)

    zeros_c = jnp.zeros((C, LANES), F32)

    probs = [(c, p) for c in range(nchunk) for p in range(npair)]
    chunks = range(len(probs))
    sl = [(slice(c * C, (c + 1) * C), slice(p * LANES, (p + 1) * LANES)) for c, p in probs]
    rt = [rt_all[s] for s in sl]
    bt = [bt_all[s] for s in sl]
    at = [at_all[s] for s in sl]
    kt = [kt_all[s] for s in sl]
    vc = [v[s] for s in sl]
    p_last = [e_cum[(c + 1) * C - 1:(c + 1) * C, p * LANES:(p + 1) * LANES] for c, p in probs]

    gram = []
    for c in chunks:
        lhs = jnp.concatenate([stack(bt[c]), stack(rt[c])], axis=0).astype(BF16)
        rhs = jnp.concatenate([stack(at[c]), stack(kt[c])], axis=0).astype(BF16)
        gram.append(jnp.where(gram_mask, _dot_nt(lhs, rhs), 0.0))
    n_ab = [gm[:C2, :C2] for gm in gram]
    a_rows = [gm[C2:].astype(BF16) for gm in gram]
    sv = [stack(v_c).astype(BF16) for v_c in vc]
    x0 = [unstack(_dot(gram[c][:C2, C2:].astype(BF16), sv[c])) for c in chunks]

    tinv = [jnp.where(eye, 1.0, n) for n in n_ab]
    pw = [_dot(n.astype(BF16), n.astype(BF16)) for n in n_ab]
    for lvl in range(2, int(math.log2(C)) + 1):
        last = lvl == int(math.log2(C))
        rhs = [tinv[c].astype(BF16) if last else
               jnp.concatenate([tinv[c], pw[c]], axis=1).astype(BF16) for c in chunks]
        prod = [_dot(pw[c].astype(BF16), rhs[c]) for c in chunks]
        tinv = [tinv[c] + prod[c][:, :C2] for c in chunks]
        if not last:
            pw = [prod[c][:, C2:] for c in chunks]

    tb = [_dot(tinv[c].astype(BF16),
               jnp.concatenate([stack(bt[c]), stack(x0[c])], axis=1).astype(BF16)) for c in chunks]
    w_c = [unstack(m[:, :LANES]) for m in tb]
    u0 = [unstack(m[:, LANES:]) for m in tb]
    zeros_s = jnp.zeros((C2, LANES), BF16)
    ab = [_dot(a_rows[c], jnp.concatenate(
              [jnp.concatenate([stack(w_c[c]), stack(u0[c])], axis=1).astype(BF16),
               jnp.concatenate([zeros_s, sv[c]], axis=1)], axis=0)) for c in chunks]
    r_eff = [(rt[c] + unstack(ab[c][:, :LANES])).astype(BF16) for c in chunks]
    y0 = [unstack(ab[c][:, LANES:]) for c in chunks]
    tn = [_dot_tn(jnp.concatenate([jnp.concatenate([w_c[c], u0[c]], axis=1),
                                   jnp.concatenate([zeros_c, vc[c]], axis=1)], axis=0).astype(BF16),
                  jnp.concatenate([at[c], kt[c]], axis=0).astype(BF16)) for c in chunks]
    a_c = [(jnp.where(s_mask, m[:C2], 0.0) * p_last[c]).astype(BF16) for c, m in enumerate(tn)]
    b_c = [jnp.where(s_mask, m[C2:], 0.0) * p_last[c] for c, m in enumerate(tn)]

    s = [s_ref[p] for p in range(npair)]
    ys = []
    for i, (c, p) in enumerate(probs):
        sb = s[p].astype(BF16)
        ys.append(y0[i] + _dot_nt(r_eff[i], sb))
        s[p] = s[p] * p_last[i] + _dot(sb, a_c[i]) + b_c[i]
    for p in range(npair):
        s_ref[p] = s[p]

    y = jnp.concatenate([jnp.concatenate(ys[c * npair:(c + 1) * npair], axis=1)
                         for c in range(nchunk)], axis=0)
    mean = head_sum(y) * (1.0 / HEAD_DIM)
    yc = y - mean
    var = head_sum(yc * yc) * (1.0 / HEAD_DIM)
    yn = yc * lax.rsqrt(var + LNX_EPS) * lnw_ref[...] + lnb_ref[...]
    o_ref[...] = ((yn + bonus) * g).astype(o_ref.dtype)


def _rwkv(z, mu, w_up, w0, a_up, a0, g_up, k_k, k_a, r_k, lnx_w, lnx_b, *, batch, seq, tc):
    n = z.shape[0]
    nt = seq // tc
    width = RW_PAIRS_PER_STEP * LANES
    groups = RW_WIDTH // width
    col0 = 3 * SB_WIDTH // width
    lo_blk = (3 * SB_WIDTH + 3 * RW_WIDTH) // LORA_PAD
    vec = lambda off: pl.BlockSpec((1, width), lambda b, p, t: (0, off + p))
    return pl.pallas_call(
        functools.partial(_rwkv_kernel, tc=tc),
        grid=(batch, groups, nt),
        in_specs=[
            pl.BlockSpec((tc, width), lambda b, p, t: (b * nt + t, col0 + p)),
            pl.BlockSpec((tc, width), lambda b, p, t: (b * nt + t, col0 + groups + p)),
            pl.BlockSpec((tc, width), lambda b, p, t: (b * nt + t, col0 + 2 * groups + p)),
            pl.BlockSpec((tc, LORA_PAD), lambda b, p, t: (b * nt + t, lo_blk)),
            vec(0), vec(groups), vec(2 * groups),
            pl.BlockSpec((1, LORA_PAD), lambda b, p, t: (0, 3 * RW_WIDTH // LORA_PAD)),
            pl.BlockSpec((DECAY_LORA, width), lambda b, p, t: (0, p)),
            vec(0),
            pl.BlockSpec((ICL_LORA, width), lambda b, p, t: (0, p)),
            vec(0),
            pl.BlockSpec((LORA_PAD - DECAY_LORA - ICL_LORA, width), lambda b, p, t: (0, p)),
            vec(0), vec(0), vec(0), vec(0), vec(0),
        ],
        out_specs=pl.BlockSpec((tc, width), lambda b, p, t: (b * nt + t, p)),
        out_shape=jax.ShapeDtypeStruct((n, RW_WIDTH), BF16),
        scratch_shapes=[
            pltpu.VMEM((RW_PAIRS_PER_STEP, LANES, LANES), F32),
            pltpu.VMEM((8, width), F32), pltpu.VMEM((8, width), F32), pltpu.VMEM((8, width), F32),
            pltpu.VMEM((8, LORA_PAD), F32),
        ],
        compiler_params=_params("parallel", "parallel", "arbitrary"),
        name="rwkv7",
    )(z, z, z, z, mu, mu, mu, mu, w_up, w0, a_up, a0, g_up, k_k, k_a, r_k, lnx_w, lnx_b)


def _sgu_kernel(x_ref, u_ref, v_ref, lg_ref, lb_ref, ws_ref, bs_ref, wo_ref, o_ref, y_ref, *, tm):
    v = v_ref[...]
    mean = jnp.mean(v, axis=-1, keepdims=True)
    vc = v - mean
    var = jnp.mean(vc * vc, axis=-1, keepdims=True)
    vn = (vc * lax.rsqrt(var + LN_EPS) * lg_ref[...] + lb_ref[...]).astype(BF16)

    ti = lax.broadcasted_iota(jnp.int32, (SGU_BLOCK, SGU_BLOCK), 0) // STREAM_CHUNK
    si = lax.broadcasted_iota(jnp.int32, (SGU_BLOCK, SGU_BLOCK), 1) // STREAM_CHUNK
    causal = si <= ti
    gw = v.shape[1] // SGU_GROUPS
    for g in range(SGU_GROUPS):
        ws = jnp.where(causal, ws_ref[g], 0.0).astype(BF16)
        cols = slice(g * gw, (g + 1) * gw)
        for blk in range(tm // SGU_BLOCK):
            rows = slice(blk * SGU_BLOCK, (blk + 1) * SGU_BLOCK)
            mixed = _dot(ws, vn[rows, cols]) + bs_ref[:, g:g + 1]
            y_ref[rows, cols] = (u_ref[rows, cols] * mixed).astype(BF16)
    o_ref[...] = x_ref[...] + _dot(y_ref[...], wo_ref[...])


def _sgu(x, zc, ln_g, ln_b, w_s, b_s_t, w_out, *, tm):
    n, d = x.shape
    return pl.pallas_call(
        functools.partial(_sgu_kernel, tm=tm),
        grid=(n // tm,),
        in_specs=[
            pl.BlockSpec((tm, d), lambda i: (i, 0)),
            pl.BlockSpec((tm, d), lambda i: (i, 0)),
            pl.BlockSpec((tm, d), lambda i: (i, 1)),
            pl.BlockSpec((1, d), lambda i: (0, 0)),
            pl.BlockSpec((1, d), lambda i: (0, 0)),
            pl.BlockSpec((SGU_GROUPS, SGU_BLOCK, SGU_BLOCK), lambda i: (0, 0, 0)),
            pl.BlockSpec((SGU_BLOCK, SGU_GROUPS), lambda i: (0, 0)),
            pl.BlockSpec((d, d), lambda i: (0, 0)),
        ],
        out_specs=pl.BlockSpec((tm, d), lambda i: (i, 0)),
        out_shape=jax.ShapeDtypeStruct((n, d), F32),
        scratch_shapes=[pltpu.VMEM((tm, d), BF16)],
        compiler_params=_params("parallel"),
        name="sgu",
    )(x, zc, zc, ln_g, ln_b, w_s, b_s_t, w_out)


def _pad_cols(w, width):
    return jnp.pad(w, ((0, 0), (0, width - w.shape[1])))


def _pad_rows(w, height):
    return jnp.pad(w, ((0, height - w.shape[0]), (0, 0)))


def kernel(x, norm_mix, norm_ffn, norm_final, even_w_in, even_mu, even_w_up, even_w0, even_a_up, even_a0, even_g_up, even_k_k, even_k_a, even_r_k, even_lnx_w, even_lnx_b, even_w_out, odd_w_in, odd_ln_g, odd_ln_b, odd_w_s, odd_b_s, odd_w_out, ffn_w1, ffn_w2):
    batch, seq, d = x.shape
    depth = norm_mix.shape[0]
    assert (batch * seq) % ROW_TILE == 0 and seq % SB_QUERY_TILE == 0 and seq % RW_TIME_TILE == 0
    assert even_w_in.shape[2] == 3 * SB_WIDTH + 3 * RW_WIDTH + even_g_up.shape[1] + DECAY_LORA + ICL_LORA
    xf = x.reshape(batch * seq, d)
    row = lambda a: a.reshape(1, -1)

    rw_cols = 3 * RW_WIDTH + LORA_PAD
    in_cols = 3 * SB_WIDTH + rw_cols + LANES

    for layer in range(depth):
        i = layer // 2
        if layer % 2 == 0:
            w_in = _pad_cols(even_w_in[i], in_cols).astype(BF16)
            z = _norm_matmul(xf, row(norm_mix[layer]), w_in, tm=ROW_TILE, tn=in_cols // IN_PROJ_COL_STEPS)
            o_sb = _sb_attention(z, batch=batch, seq=seq, tq=SB_QUERY_TILE)
            o_rw = _rwkv(z, _pad_cols(row(even_mu[i]), rw_cols),
                         even_w_up[i].astype(BF16), row(even_w0[i]),
                         even_a_up[i].astype(BF16), row(even_a0[i]),
                         _pad_rows(even_g_up[i], LORA_PAD - DECAY_LORA - ICL_LORA).astype(BF16),
                         row(even_k_k[i]), row(even_k_a[i]),
                         row(even_r_k[i]), row(even_lnx_w[i]), row(even_lnx_b[i]),
                         batch=batch, seq=seq, tc=RW_TIME_TILE)
            xf = _proj2(xf, o_sb, o_rw, even_w_out[i].astype(BF16), tm=ROW_TILE)
        else:
            zc = _norm_matmul(xf, row(norm_mix[layer]), odd_w_in[i].astype(BF16),
                              tm=ROW_TILE, tn=GELU_COL_TILE, gelu=True)
            xf = _sgu(xf, zc, row(odd_ln_g[i]), row(odd_ln_b[i]), odd_w_s[i],
                      odd_b_s[i].T, odd_w_out[i].astype(BF16), tm=SGU_ROW_TILE)
        xf = _ffn(xf, row(norm_ffn[layer]), ffn_w1[layer].astype(BF16), ffn_w2[layer].astype(BF16),
                  row(norm_final), tm=ROW_TILE, th=FFN_HIDDEN_TILE, final_norm=(layer == depth - 1))
    return xf.reshape(batch, seq, d)
```

```python
import functools
import math

import jax
import jax.numpy as jnp
from jax import lax
from jax.experimental import pallas as pl
from jax.experimental.pallas import tpu as pltpu

F32 = jnp.float32
BF16 = jnp.bfloat16

LANES = 128
HEAD_DIM = 64
RMS_EPS = 1e-6
LN_EPS = 1e-5
LNX_EPS = 64e-5
SB_WIDTH = 1024
RW_WIDTH = 1024
DECAY_LORA = 64
ICL_LORA = 64
LORA_PAD = 384
SGU_GROUPS = 8
SGU_BLOCK = 128
STREAM_CHUNK = 64
RW_CHUNK = 64
RW_PAIRS_PER_STEP = 2

VMEM_LIMIT = 56 * 1024 * 1024
ROW_TILE = 512
FFN_HIDDEN_TILE = 2048
IN_PROJ_COL_STEPS = 2
GELU_COL_TILE = 2048
SGU_ROW_TILE = 256
SB_QUERY_TILE = 256
SB_TILES_PER_STEP = 2
RW_TIME_TILE = 512


def _params(*sem):
    return pltpu.CompilerParams(dimension_semantics=sem, vmem_limit_bytes=VMEM_LIMIT)


def _rms(x, g):
    return x * lax.rsqrt(jnp.mean(x * x, axis=-1, keepdims=True) + RMS_EPS) * g


def _dot(a, b):
    return jnp.dot(a, b, preferred_element_type=F32)


def _dot_nt(a, b):
    return lax.dot_general(a, b, (((1,), (1,)), ((), ())), preferred_element_type=F32)


def _dot_tn(a, b):
    return lax.dot_general(a, b, (((0,), (0,)), ((), ())), preferred_element_type=F32)


def _split_bf16(x):
    hi = x.astype(BF16)
    lo = (x - hi.astype(F32)).astype(BF16)
    return hi, lo


def _norm_matmul_kernel(x_ref, g_ref, w_ref, o_ref, h_ref, *, gelu):
    @pl.when(pl.program_id(1) == 0)
    def _():
        h_ref[...] = _rms(x_ref[...], g_ref[...]).astype(BF16)

    y = _dot(h_ref[...], w_ref[...])
    if gelu:
        y = 0.5 * y * (1.0 + lax.erf(y * math.sqrt(0.5)))
    o_ref[...] = y.astype(o_ref.dtype)


def _norm_matmul(x, g, w, *, tm, tn, gelu=False, out_dtype=F32):
    n, d = x.shape
    nout = w.shape[1]
    return pl.pallas_call(
        functools.partial(_norm_matmul_kernel, gelu=gelu),
        grid=(n // tm, nout // tn),
        in_specs=[
            pl.BlockSpec((tm, d), lambda i, j: (i, 0)),
            pl.BlockSpec((1, d), lambda i, j: (0, 0)),
            pl.BlockSpec((d, tn), lambda i, j: (0, j)),
        ],
        out_specs=pl.BlockSpec((tm, tn), lambda i, j: (i, j)),
        out_shape=jax.ShapeDtypeStruct((n, nout), out_dtype),
        scratch_shapes=[pltpu.VMEM((tm, d), BF16)],
        compiler_params=_params("parallel", "arbitrary"),
        name="norm_matmul_gelu" if gelu else "norm_matmul",
    )(x, g, w)


def _ffn_kernel(x_ref, g_ref, w1_ref, w2_ref, gf_ref, o_ref, h_ref, *, final_norm):
    j = pl.program_id(1)

    @pl.when(j == 0)
    def _():
        x = x_ref[...]
        h_ref[...] = _rms(x, g_ref[...]).astype(BF16)
        o_ref[...] = x

    a = jnp.square(jnp.maximum(_dot(h_ref[...], w1_ref[...]), 0.0)).astype(BF16)
    o_ref[...] += _dot(a, w2_ref[...])

    if final_norm:
        @pl.when(j == pl.num_programs(1) - 1)
        def _():
            o_ref[...] = _rms(o_ref[...], gf_ref[...])


def _ffn(x, g, w1, w2, gf, *, tm, th, final_norm):
    n, d = x.shape
    hid = w1.shape[1]
    return pl.pallas_call(
        functools.partial(_ffn_kernel, final_norm=final_norm),
        grid=(n // tm, hid // th),
        in_specs=[
            pl.BlockSpec((tm, d), lambda i, j: (i, 0)),
            pl.BlockSpec((1, d), lambda i, j: (0, 0)),
            pl.BlockSpec((d, th), lambda i, j: (0, j)),
            pl.BlockSpec((th, d), lambda i, j: (j, 0)),
            pl.BlockSpec((1, d), lambda i, j: (0, 0)),
        ],
        out_specs=pl.BlockSpec((tm, d), lambda i, j: (i, 0)),
        out_shape=jax.ShapeDtypeStruct((n, d), F32),
        scratch_shapes=[pltpu.VMEM((tm, d), BF16)],
        compiler_params=_params("parallel", "arbitrary"),
        name="ffn_final" if final_norm else "ffn",
    )(x, g, w1, w2, gf)


def _proj2_kernel(x_ref, a_ref, b_ref, w_ref, o_ref):
    ka = a_ref.shape[1]
    o_ref[...] = (x_ref[...] + _dot(a_ref[...], w_ref[:ka, :]) + _dot(b_ref[...], w_ref[ka:, :]))


def _proj2(x, a, b, w, *, tm):
    n, d = x.shape
    ka, kb = a.shape[1], b.shape[1]
    return pl.pallas_call(
        _proj2_kernel,
        grid=(n // tm,),
        in_specs=[
            pl.BlockSpec((tm, d), lambda i: (i, 0)),
            pl.BlockSpec((tm, ka), lambda i: (i, 0)),
            pl.BlockSpec((tm, kb), lambda i: (i, 0)),
            pl.BlockSpec((ka + kb, d), lambda i: (0, 0)),
        ],
        out_specs=pl.BlockSpec((tm, d), lambda i: (i, 0)),
        out_shape=jax.ShapeDtypeStruct((n, d), F32),
        compiler_params=_params("parallel"),
        name="out_proj",
    )(x, a, b, w)


NEG_BIG = -1e30
SB_KEY_TILE = 128
DEAD_LOG2_WEIGHT = -150.5


def _sb_kernel(q_ref, k_ref, v_ref, o_ref, z_ref, s_ref, ls_ref, e_ref, rs_ref, c_ref, acc_ref, *, tq):
    tk = SB_KEY_TILE
    per_q = tq // tk
    assert per_q == 2
    lane = lax.broadcasted_iota(jnp.int32, (tq, LANES), 1)
    head0 = lane < HEAD_DIM

    jj = lax.broadcasted_iota(jnp.int32, (tk, 2 * tk), 0)
    ss = lax.broadcasted_iota(jnp.int32, (tk, 2 * tk), 1)
    tail_mat = jnp.where((ss >= tk) | (jj > ss), -1.0, 0.0).astype(BF16)

    def walker(sub):
        qi = pl.program_id(2) * SB_TILES_PER_STEP + sub
        n_tiles = (qi + 1) * per_q
        q_rows = slice(sub * tq, (sub + 1) * tq)
        z_s, s_s, ls_s, e_s, rs_s, c_s, acc_s = (r.at[sub] for r in
                                                 (z_ref, s_ref, ls_ref, e_ref, rs_ref, c_ref, acc_ref))
        q = q_ref[q_rows, :] * (HEAD_DIM ** -0.5 * math.log2(math.e))
        qh = (jnp.where(head0, q, 0.0).astype(BF16), jnp.where(head0, 0.0, q).astype(BF16))

        def key_tile(m):
            return jnp.maximum(n_tiles - 1 - m, 0)

        def tile_rows(m):
            return pl.ds(pl.multiple_of(key_tile(m) * tk, tk), tk)

        def stage_a1(m, slot):
            kb = k_ref[tile_rows(m), :].astype(BF16)
            for h in range(2):
                z_s[h, slot] = _dot_nt(qh[h], kb)

        def stage_a2(slot, masked_tile=None, guard_tile=None):
            for h in range(2):
                z = z_s[h, slot]
                sp = jnp.maximum(z, 0.0) + jnp.log2(1.0 + jnp.exp2(-jnp.abs(z)))
                log_beta = z - sp
                if guard_tile is not None:
                    sp = jnp.where(guard_tile < n_tiles, sp, 0.0)
                    log_beta = jnp.where(guard_tile < n_tiles, log_beta, NEG_BIG)
                if masked_tile is not None:
                    t_pos = qi * tq + lax.broadcasted_iota(jnp.int32, (tq, tk), 0)
                    s_pos = key_tile(masked_tile) * tk + lax.broadcasted_iota(jnp.int32, (tq, tk), 1)
                    sp = jnp.where(s_pos < t_pos, sp, 0.0)
                    log_beta = jnp.where(s_pos < t_pos, log_beta, NEG_BIG)
                s_s[h, slot] = sp.astype(BF16)
                ls_s[h, slot] = log_beta

        def stage_b(slot):
            for h in range(2):
                both = _dot(s_s[h, slot], tail_mat)
                e_s[h, slot] = ls_s[h, slot] + both[:, :tk]
                rs_s[h, slot] = both[:, tk:]

        def stage_c(m, slot):
            vb = v_ref[tile_rows(m), :].astype(BF16)
            for h in range(2):
                c = c_s[h]
                w = jnp.exp2(e_s[h, slot] + c)
                acc_s[h] += _dot(w.astype(BF16), vb)
                c_s[h] = c + rs_s[h, slot]

        def two_steps(state, guarded=False):
            pair, _ = state
            i = 2 * pair + 3
            for s_new, s_old in ((1, 0), (0, 1)):
                stage_a1(i, s_new)
                stage_b(s_new)
                stage_c(i - 3, s_old)
                stage_a2(s_old, guard_tile=i - 1 if guarded else None)
                i = i + 1
            return pair + 1, jnp.max(c_s[...]) >= DEAD_LOG2_WEIGHT

        def straight():
            c_s[...] = jnp.zeros((2, tq, tk), F32)
            acc_s[...] = jnp.zeros((2, tq, LANES), F32)
            stage_a1(0, 0)
            stage_a2(0, masked_tile=0)
            stage_a1(1, 1)
            stage_b(0)
            stage_a2(1, masked_tile=1)
            stage_a1(2, 0)
            return two_steps(two_steps((0, None), guarded=True), guarded=True)

        def rest(state):
            lax.while_loop(lambda st: (st[0] < n_tiles // 2) & st[1], two_steps,
                           (jnp.int32(state[0]), state[1]))
            o_ref[q_rows, :] = jnp.where(head0, acc_s[0], acc_s[1]).astype(o_ref.dtype)

        return straight, rest

    walkers = [walker(sub) for sub in range(SB_TILES_PER_STEP)]
    states = [straight() for straight, _ in walkers]
    for (_, rest), state in zip(walkers, states):
        rest(state)


def _sb_attention(z, *, batch, seq, tq):
    n = z.shape[0]
    rows = SB_TILES_PER_STEP * tq
    nq = seq // rows
    pairs = SB_WIDTH // LANES
    sub = SB_TILES_PER_STEP
    return pl.pallas_call(
        functools.partial(_sb_kernel, tq=tq),
        grid=(batch, pairs, nq),
        in_specs=[
            pl.BlockSpec((rows, LANES), lambda b, p, i: (b * nq + i, p)),
            pl.BlockSpec((seq, LANES), lambda b, p, i: (b, pairs + p)),
            pl.BlockSpec((seq, LANES), lambda b, p, i: (b, 2 * pairs + p)),
        ],
        out_specs=pl.BlockSpec((rows, LANES), lambda b, p, i: (b * nq + i, p)),
        out_shape=jax.ShapeDtypeStruct((n, SB_WIDTH), BF16),
        scratch_shapes=[
            pltpu.VMEM((sub, 2, 2, tq, SB_KEY_TILE), F32),
            pltpu.VMEM((sub, 2, 2, tq, SB_KEY_TILE), BF16),
            pltpu.VMEM((sub, 2, 2, tq, SB_KEY_TILE), F32),
            pltpu.VMEM((sub, 2, 2, tq, SB_KEY_TILE), F32),
            pltpu.VMEM((sub, 2, 2, tq, SB_KEY_TILE), F32),
            pltpu.VMEM((sub, 2, tq, SB_KEY_TILE), F32),
            pltpu.VMEM((sub, 2, tq, LANES), F32),
        ],
        compiler_params=_params("parallel", "parallel", "arbitrary"),
        name="stick_breaking",
    )(z, z, z)


def _rwkv_kernel(r_ref, k_ref, v_ref, lo_ref, mur_ref, muk_ref, muv_ref, mulo_ref,
                 wup_ref, w0_ref, aup_ref, a0_ref, gup_ref, kk_ref, ka_ref, rk_ref,
                 lnw_ref, lnb_ref, o_ref, s_ref, pr_ref, pk_ref, pv_ref, plo_ref, *, tc):
    C = RW_CHUNK
    C2 = 2 * C
    nchunk = tc // C
    npair = RW_PAIRS_PER_STEP
    width = npair * LANES
    t = pl.program_id(2)

    @pl.when(t == 0)
    def _():
        s_ref[...] = jnp.zeros_like(s_ref)
        pr_ref[...] = jnp.zeros_like(pr_ref)
        pk_ref[...] = jnp.zeros_like(pk_ref)
        pv_ref[...] = jnp.zeros_like(pv_ref)
        plo_ref[...] = jnp.zeros_like(plo_ref)

    def shifted(z_ref, prev_ref, mu_ref):
        z = z_ref[...]
        row = lax.broadcasted_iota(jnp.int32, z.shape, 0)
        prev = jnp.where(row == 0, prev_ref[0:1, :], pltpu.roll(z, 1, axis=0))
        prev_ref[0:1, :] = z[tc - 1:tc, :]
        return z + (prev - z) * mu_ref[...]

    r = shifted(r_ref, pr_ref, mur_ref)
    k = shifted(k_ref, pk_ref, muk_ref)
    v = shifted(v_ref, pv_ref, muv_ref)
    lo = shifted(lo_ref, plo_ref, mulo_ref)
    xw = lo[:, :DECAY_LORA]
    xa = lo[:, DECAY_LORA:DECAY_LORA + ICL_LORA]
    xg = lo[:, DECAY_LORA + ICL_LORA:]

    wl = w0_ref[...] + _dot(jnp.tanh(xw).astype(BF16), wup_ref[...])
    w_log = -(jnp.maximum(-wl, 0.0) + jnp.log1p(jnp.exp(-jnp.abs(wl)))) - 0.5
    lw = -jnp.exp(w_log)
    a = jax.nn.sigmoid(a0_ref[...] + _dot(xa.astype(BF16), aup_ref[...]))
    g = _dot(jax.nn.sigmoid(xg).astype(BF16), gup_ref[...])

    li = lax.broadcasted_iota(jnp.int32, (width, width), 0)
    lj = lax.broadcasted_iota(jnp.int32, (width, width), 1)
    head_ones = jnp.where(li // HEAD_DIM == lj // HEAD_DIM, 1.0, 0.0).astype(BF16)

    def head_sum(x):
        return _dot(x.astype(BF16), head_ones)

    kk = k * kk_ref[...]
    kk = kk / jnp.maximum(jnp.sqrt(head_sum(kk * kk)), 1e-12)
    kmod = k * (1.0 + (a - 1.0) * ka_ref[...])
    bonus = head_sum(r * kmod * rk_ref[...]) * v
    kka = kk * a

    lane = lax.broadcasted_iota(jnp.int32, (C, LANES), 1)
    head0 = lane < HEAD_DIM

    def stack(x):
        return jnp.concatenate([jnp.where(head0, x, 0.0), jnp.where(head0, 0.0, x)], axis=0)

    def unstack(x):
        return x[:C] + x[C:]

    ci = lax.broadcasted_iota(jnp.int32, (C, C2), 0)
    cj = lax.broadcasted_iota(jnp.int32, (C, C2), 1) & (C - 1)
    cum_mat = jnp.where(cj <= ci, 1.0, 0.0).astype(BF16)
    lw_hi, lw_lo = _split_bf16(lw)
    cum = jnp.concatenate(
        [_dot(cum_mat, jnp.concatenate([lw_hi[c * C:(c + 1) * C], lw_lo[c * C:(c + 1) * C]], axis=0))
         for c in range(nchunk)], axis=0)
    e_cum = jnp.exp(cum)
    e_inv = jnp.exp(-cum)
    rt_all = r * e_cum
    bt_all = kk * jnp.exp(cum - lw)
    at_all = -kka * e_inv
    kt_all = kmod * e_inv

    gi = lax.broadcasted_iota(jnp.int32, (2 * C2, 2 * C2), 0)
    gj = lax.broadcasted_iota(jnp.int32, (2 * C2, 2 * C2), 1)
    ti, tj = gi & (C - 1), gj & (C - 1)
    gram_mask = tj < ti + jnp.where(gi < C2, 0, 1)
    si = lax.broadcasted_iota(jnp.int32, (C2, C2), 0)
    sj = lax.broadcasted_iota(jnp.int32, (C2, C2), 1)
    eye = si == sj
    s_mask = (si // C) == (sj // C)

    zeros_c = jnp.zeros((C, LANES), F32)

    probs = [(c, p) for c in range(nchunk) for p in range(npair)]
    chunks = range(len(probs))
    sl = [(slice(c * C, (c + 1) * C), slice(p * LANES, (p + 1) * LANES)) for c, p in probs]
    rt = [rt_all[s] for s in sl]
    bt = [bt_all[s] for s in sl]
    at = [at_all[s] for s in sl]
    kt = [kt_all[s] for s in sl]
    vc = [v[s] for s in sl]
    p_last = [e_cum[(c + 1) * C - 1:(c + 1) * C, p * LANES:(p + 1) * LANES] for c, p in probs]

    gram = []
    for c in chunks:
        lhs = jnp.concatenate([stack(bt[c]), stack(rt[c])], axis=0).astype(BF16)
        rhs = jnp.concatenate([stack(at[c]), stack(kt[c])], axis=0).astype(BF16)
        gram.append(jnp.where(gram_mask, _dot_nt(lhs, rhs), 0.0))
    n_ab = [gm[:C2, :C2] for gm in gram]
    a_rows = [gm[C2:].astype(BF16) for gm in gram]
    sv = [stack(v_c).astype(BF16) for v_c in vc]
    x0 = [unstack(_dot(gram[c][:C2, C2:].astype(BF16), sv[c])) for c in chunks]

    tinv = [jnp.where(eye, 1.0, n) for n in n_ab]
    pw = [_dot(n.astype(BF16), n.astype(BF16)) for n in n_ab]
    for lvl in range(2, int(math.log2(C)) + 1):
        last = lvl == int(math.log2(C))
        rhs = [tinv[c].astype(BF16) if last else
               jnp.concatenate([tinv[c], pw[c]], axis=1).astype(BF16) for c in chunks]
        prod = [_dot(pw[c].astype(BF16), rhs[c]) for c in chunks]
        tinv = [tinv[c] + prod[c][:, :C2] for c in chunks]
        if not last:
            pw = [prod[c][:, C2:] for c in chunks]

    tb = [_dot(tinv[c].astype(BF16),
               jnp.concatenate([stack(bt[c]), stack(x0[c])], axis=1).astype(BF16)) for c in chunks]
    w_c = [unstack(m[:, :LANES]) for m in tb]
    u0 = [unstack(m[:, LANES:]) for m in tb]
    zeros_s = jnp.zeros((C2, LANES), BF16)
    ab = [_dot(a_rows[c], jnp.concatenate(
              [jnp.concatenate([stack(w_c[c]), stack(u0[c])], axis=1).astype(BF16),
               jnp.concatenate([zeros_s, sv[c]], axis=1)], axis=0)) for c in chunks]
    r_eff = [(rt[c] + unstack(ab[c][:, :LANES])).astype(BF16) for c in chunks]
    y0 = [unstack(ab[c][:, LANES:]) for c in chunks]
    tn = [_dot_tn(jnp.concatenate([jnp.concatenate([w_c[c], u0[c]], axis=1),
                                   jnp.concatenate([zeros_c, vc[c]], axis=1)], axis=0).astype(BF16),
                  jnp.concatenate([at[c], kt[c]], axis=0).astype(BF16)) for c in chunks]
    a_c = [(jnp.where(s_mask, m[:C2], 0.0) * p_last[c]).astype(BF16) for c, m in enumerate(tn)]
    b_c = [jnp.where(s_mask, m[C2:], 0.0) * p_last[c] for c, m in enumerate(tn)]

    s = [s_ref[p] for p in range(npair)]
    ys = []
    for i, (c, p) in enumerate(probs):
        sb = s[p].astype(BF16)
        ys.append(y0[i] + _dot_nt(r_eff[i], sb))
        s[p] = s[p] * p_last[i] + _dot(sb, a_c[i]) + b_c[i]
    for p in range(npair):
        s_ref[p] = s[p]

    y = jnp.concatenate([jnp.concatenate(ys[c * npair:(c + 1) * npair], axis=1)
                         for c in range(nchunk)], axis=0)
    mean = head_sum(y) * (1.0 / HEAD_DIM)
    yc = y - mean
    var = head_sum(yc * yc) * (1.0 / HEAD_DIM)
    yn = yc * lax.rsqrt(var + LNX_EPS) * lnw_ref[...] + lnb_ref[...]
    o_ref[...] = ((yn + bonus) * g).astype(o_ref.dtype)


def _rwkv(z, mu, w_up, w0, a_up, a0, g_up, k_k, k_a, r_k, lnx_w, lnx_b, *, batch, seq, tc):
    n = z.shape[0]
    nt = seq // tc
    width = RW_PAIRS_PER_STEP * LANES
    groups = RW_WIDTH // width
    col0 = 3 * SB_WIDTH // width
    lo_blk = (3 * SB_WIDTH + 3 * RW_WIDTH) // LORA_PAD
    vec = lambda off: pl.BlockSpec((1, width), lambda b, p, t: (0, off + p))
    return pl.pallas_call(
        functools.partial(_rwkv_kernel, tc=tc),
        grid=(batch, groups, nt),
        in_specs=[
            pl.BlockSpec((tc, width), lambda b, p, t: (b * nt + t, col0 + p)),
            pl.BlockSpec((tc, width), lambda b, p, t: (b * nt + t, col0 + groups + p)),
            pl.BlockSpec((tc, width), lambda b, p, t: (b * nt + t, col0 + 2 * groups + p)),
            pl.BlockSpec((tc, LORA_PAD), lambda b, p, t: (b * nt + t, lo_blk)),
            vec(0), vec(groups), vec(2 * groups),
            pl.BlockSpec((1, LORA_PAD), lambda b, p, t: (0, 3 * RW_WIDTH // LORA_PAD)),
            pl.BlockSpec((DECAY_LORA, width), lambda b, p, t: (0, p)),
            vec(0),
            pl.BlockSpec((ICL_LORA, width), lambda b, p, t: (0, p)),
            vec(0),
            pl.BlockSpec((LORA_PAD - DECAY_LORA - ICL_LORA, width), lambda b, p, t: (0, p)),
            vec(0), vec(0), vec(0), vec(0), vec(0),
        ],
        out_specs=pl.BlockSpec((tc, width), lambda b, p, t: (b * nt + t, p)),
        out_shape=jax.ShapeDtypeStruct((n, RW_WIDTH), BF16),
        scratch_shapes=[
            pltpu.VMEM((RW_PAIRS_PER_STEP, LANES, LANES), F32),
            pltpu.VMEM((8, width), F32), pltpu.VMEM((8, width), F32), pltpu.VMEM((8, width), F32),
            pltpu.VMEM((8, LORA_PAD), F32),
        ],
        compiler_params=_params("parallel", "parallel", "arbitrary"),
        name="rwkv7",
    )(z, z, z, z, mu, mu, mu, mu, w_up, w0, a_up, a0, g_up, k_k, k_a, r_k, lnx_w, lnx_b)


def _sgu_kernel(x_ref, u_ref, v_ref, lg_ref, lb_ref, ws_ref, bs_ref, wo_ref, o_ref, y_ref, *, tm):
    v = v_ref[...]
    mean = jnp.mean(v, axis=-1, keepdims=True)
    vc = v - mean
    var = jnp.mean(vc * vc, axis=-1, keepdims=True)
    vn = (vc * lax.rsqrt(var + LN_EPS) * lg_ref[...] + lb_ref[...]).astype(BF16)

    ti = lax.broadcasted_iota(jnp.int32, (SGU_BLOCK, SGU_BLOCK), 0) // STREAM_CHUNK
    si = lax.broadcasted_iota(jnp.int32, (SGU_BLOCK, SGU_BLOCK), 1) // STREAM_CHUNK
    causal = si <= ti
    gw = v.shape[1] // SGU_GROUPS
    for g in range(SGU_GROUPS):
        ws = jnp.where(causal, ws_ref[g], 0.0).astype(BF16)
        cols = slice(g * gw, (g + 1) * gw)
        for blk in range(tm // SGU_BLOCK):
            rows = slice(blk * SGU_BLOCK, (blk + 1) * SGU_BLOCK)
            mixed = _dot(ws, vn[rows, cols]) + bs_ref[:, g:g + 1]
            y_ref[rows, cols] = (u_ref[rows, cols] * mixed).astype(BF16)
    o_ref[...] = x_ref[...] + _dot(y_ref[...], wo_ref[...])


def _sgu(x, zc, ln_g, ln_b, w_s, b_s_t, w_out, *, tm):
    n, d = x.shape
    return pl.pallas_call(
        functools.partial(_sgu_kernel, tm=tm),
        grid=(n // tm,),
        in_specs=[
            pl.BlockSpec((tm, d), lambda i: (i, 0)),
            pl.BlockSpec((tm, d), lambda i: (i, 0)),
            pl.BlockSpec((tm, d), lambda i: (i, 1)),
            pl.BlockSpec((1, d), lambda i: (0, 0)),
            pl.BlockSpec((1, d), lambda i: (0, 0)),
            pl.BlockSpec((SGU_GROUPS, SGU_BLOCK, SGU_BLOCK), lambda i: (0, 0, 0)),
            pl.BlockSpec((SGU_BLOCK, SGU_GROUPS), lambda i: (0, 0)),
            pl.BlockSpec((d, d), lambda i: (0, 0)),
        ],
        out_specs=pl.BlockSpec((tm, d), lambda i: (i, 0)),
        out_shape=jax.ShapeDtypeStruct((n, d), F32),
        scratch_shapes=[pltpu.VMEM((tm, d), BF16)],
        compiler_params=_params("parallel"),
        name="sgu",
    )(x, zc, zc, ln_g, ln_b, w_s, b_s_t, w_out)


def _pad_cols(w, width):
    return jnp.pad(w, ((0, 0), (0, width - w.shape[1])))


def _pad_rows(w, height):
    return jnp.pad(w, ((0, height - w.shape[0]), (0, 0)))


def kernel(x, norm_mix, norm_ffn, norm_final, even_w_in, even_mu, even_w_up, even_w0, even_a_up, even_a0, even_g_up, even_k_k, even_k_a, even_r_k, even_lnx_w, even_lnx_b, even_w_out, odd_w_in, odd_ln_g, odd_ln_b, odd_w_s, odd_b_s, odd_w_out, ffn_w1, ffn_w2):
    batch, seq, d = x.shape
    depth = norm_mix.shape[0]
    assert (batch * seq) % ROW_TILE == 0 and seq % (SB_TILES_PER_STEP * SB_QUERY_TILE) == 0 and seq % RW_TIME_TILE == 0
    assert even_w_in.shape[2] == 3 * SB_WIDTH + 3 * RW_WIDTH + even_g_up.shape[1] + DECAY_LORA + ICL_LORA
    xf = x.reshape(batch * seq, d)
    row = lambda a: a.reshape(1, -1)

    rw_cols = 3 * RW_WIDTH + LORA_PAD
    in_cols = 3 * SB_WIDTH + rw_cols + LANES

    for layer in range(depth):
        i = layer // 2
        if layer % 2 == 0:
            w_in = _pad_cols(even_w_in[i], in_cols).astype(BF16)
            z = _norm_matmul(xf, row(norm_mix[layer]), w_in, tm=ROW_TILE, tn=in_cols // IN_PROJ_COL_STEPS)
            o_sb = _sb_attention(z, batch=batch, seq=seq, tq=SB_QUERY_TILE)
            o_rw = _rwkv(z, _pad_cols(row(even_mu[i]), rw_cols),
                         even_w_up[i].astype(BF16), row(even_w0[i]),
                         even_a_up[i].astype(BF16), row(even_a0[i]),
                         _pad_rows(even_g_up[i], LORA_PAD - DECAY_LORA - ICL_LORA).astype(BF16),
                         row(even_k_k[i]), row(even_k_a[i]),
                         row(even_r_k[i]), row(even_lnx_w[i]), row(even_lnx_b[i]),
                         batch=batch, seq=seq, tc=RW_TIME_TILE)
            xf = _proj2(xf, o_sb, o_rw, even_w_out[i].astype(BF16), tm=ROW_TILE)
        else:
            zc = _norm_matmul(xf, row(norm_mix[layer]), odd_w_in[i].astype(BF16),
                              tm=ROW_TILE, tn=GELU_COL_TILE, gelu=True)
            xf = _sgu(xf, zc, row(odd_ln_g[i]), row(odd_ln_b[i]), odd_w_s[i],
                      odd_b_s[i].T, odd_w_out[i].astype(BF16), tm=SGU_ROW_TILE)
        xf = _ffn(xf, row(norm_ffn[layer]), ffn_w1[layer].astype(BF16), ffn_w2[layer].astype(BF16),
                  row(norm_final), tm=ROW_TILE, th=FFN_HIDDEN_TILE, final_norm=(layer == depth - 1))
    return xf.reshape(batch, seq, d)
```

```python
import functools
import math

import jax
import jax.numpy as jnp
from jax import lax
from jax.experimental import pallas as pl
from jax.experimental.pallas import tpu as pltpu

F32 = jnp.float32
BF16 = jnp.bfloat16

LANES = 128
HEAD_DIM = 64
RMS_EPS = 1e-6
LN_EPS = 1e-5
LNX_EPS = 64e-5
SB_WIDTH = 1024
RW_WIDTH = 1024
DECAY_LORA = 64
ICL_LORA = 64
LORA_PAD = 384
SGU_GROUPS = 8
SGU_BLOCK = 128
STREAM_CHUNK = 64
RW_CHUNK = 64
RW_PAIRS_PER_STEP = 2

VMEM_LIMIT = 56 * 1024 * 1024
ROW_TILE = 512
FFN_HIDDEN_TILE = 2048
IN_PROJ_COL_STEPS = 2
GELU_COL_TILE = 2048
SGU_ROW_TILE = 256
SB_QUERY_TILE = 256
SB_TILES_PER_STEP = 4
RW_TIME_TILE = 512


def _params(*sem):
    return pltpu.CompilerParams(dimension_semantics=sem, vmem_limit_bytes=VMEM_LIMIT)


def _rms(x, g):
    return x * lax.rsqrt(jnp.mean(x * x, axis=-1, keepdims=True) + RMS_EPS) * g


def _dot(a, b):
    return jnp.dot(a, b, preferred_element_type=F32)


def _dot_nt(a, b):
    return lax.dot_general(a, b, (((1,), (1,)), ((), ())), preferred_element_type=F32)


def _dot_tn(a, b):
    return lax.dot_general(a, b, (((0,), (0,)), ((), ())), preferred_element_type=F32)


def _split_bf16(x):
    hi = x.astype(BF16)
    lo = (x - hi.astype(F32)).astype(BF16)
    return hi, lo


def _norm_matmul_kernel(x_ref, g_ref, w_ref, o_ref, h_ref, *, gelu):
    @pl.when(pl.program_id(1) == 0)
    def _():
        h_ref[...] = _rms(x_ref[...], g_ref[...]).astype(BF16)

    y = _dot(h_ref[...], w_ref[...])
    if gelu:
        y = 0.5 * y * (1.0 + lax.erf(y * math.sqrt(0.5)))
    o_ref[...] = y.astype(o_ref.dtype)


def _norm_matmul(x, g, w, *, tm, tn, gelu=False, out_dtype=F32):
    n, d = x.shape
    nout = w.shape[1]
    return pl.pallas_call(
        functools.partial(_norm_matmul_kernel, gelu=gelu),
        grid=(n // tm, nout // tn),
        in_specs=[
            pl.BlockSpec((tm, d), lambda i, j: (i, 0)),
            pl.BlockSpec((1, d), lambda i, j: (0, 0)),
            pl.BlockSpec((d, tn), lambda i, j: (0, j)),
        ],
        out_specs=pl.BlockSpec((tm, tn), lambda i, j: (i, j)),
        out_shape=jax.ShapeDtypeStruct((n, nout), out_dtype),
        scratch_shapes=[pltpu.VMEM((tm, d), BF16)],
        compiler_params=_params("parallel", "arbitrary"),
        name="norm_matmul_gelu" if gelu else "norm_matmul",
    )(x, g, w)


def _ffn_kernel(x_ref, g_ref, w1_ref, w2_ref, gf_ref, o_ref, h_ref, *, final_norm):
    j = pl.program_id(1)

    @pl.when(j == 0)
    def _():
        x = x_ref[...]
        h_ref[...] = _rms(x, g_ref[...]).astype(BF16)
        o_ref[...] = x

    a = jnp.square(jnp.maximum(_dot(h_ref[...], w1_ref[...]), 0.0)).astype(BF16)
    o_ref[...] += _dot(a, w2_ref[...])

    if final_norm:
        @pl.when(j == pl.num_programs(1) - 1)
        def _():
            o_ref[...] = _rms(o_ref[...], gf_ref[...])


def _ffn(x, g, w1, w2, gf, *, tm, th, final_norm):
    n, d = x.shape
    hid = w1.shape[1]
    return pl.pallas_call(
        functools.partial(_ffn_kernel, final_norm=final_norm),
        grid=(n // tm, hid // th),
        in_specs=[
            pl.BlockSpec((tm, d), lambda i, j: (i, 0)),
            pl.BlockSpec((1, d), lambda i, j: (0, 0)),
            pl.BlockSpec((d, th), lambda i, j: (0, j)),
            pl.BlockSpec((th, d), lambda i, j: (j, 0)),
            pl.BlockSpec((1, d), lambda i, j: (0, 0)),
        ],
        out_specs=pl.BlockSpec((tm, d), lambda i, j: (i, 0)),
        out_shape=jax.ShapeDtypeStruct((n, d), F32),
        scratch_shapes=[pltpu.VMEM((tm, d), BF16)],
        compiler_params=_params("parallel", "arbitrary"),
        name="ffn_final" if final_norm else "ffn",
    )(x, g, w1, w2, gf)


def _proj2_kernel(x_ref, a_ref, b_ref, w_ref, o_ref):
    ka = a_ref.shape[1]
    o_ref[...] = (x_ref[...] + _dot(a_ref[...], w_ref[:ka, :]) + _dot(b_ref[...], w_ref[ka:, :]))


def _proj2(x, a, b, w, *, tm):
    n, d = x.shape
    ka, kb = a.shape[1], b.shape[1]
    return pl.pallas_call(
        _proj2_kernel,
        grid=(n // tm,),
        in_specs=[
            pl.BlockSpec((tm, d), lambda i: (i, 0)),
            pl.BlockSpec((tm, ka), lambda i: (i, 0)),
            pl.BlockSpec((tm, kb), lambda i: (i, 0)),
            pl.BlockSpec((ka + kb, d), lambda i: (0, 0)),
        ],
        out_specs=pl.BlockSpec((tm, d), lambda i: (i, 0)),
        out_shape=jax.ShapeDtypeStruct((n, d), F32),
        compiler_params=_params("parallel"),
        name="out_proj",
    )(x, a, b, w)


NEG_BIG = -1e30
SB_KEY_TILE = 128
DEAD_LOG2_WEIGHT = -150.5


def _sb_kernel(q_ref, k_ref, v_ref, o_ref, z_ref, s_ref, ls_ref, e_ref, rs_ref, c_ref, acc_ref, *, tq):
    tk = SB_KEY_TILE
    per_q = tq // tk
    assert per_q == 2
    lane = lax.broadcasted_iota(jnp.int32, (tq, LANES), 1)
    head0 = lane < HEAD_DIM

    jj = lax.broadcasted_iota(jnp.int32, (tk, 2 * tk), 0)
    ss = lax.broadcasted_iota(jnp.int32, (tk, 2 * tk), 1)
    tail_mat = jnp.where((ss >= tk) | (jj > ss), -1.0, 0.0).astype(BF16)

    def walker(sub):
        qi = pl.program_id(2) * SB_TILES_PER_STEP + sub
        n_tiles = (qi + 1) * per_q
        q_rows = slice(sub * tq, (sub + 1) * tq)
        z_s, s_s, ls_s, e_s, rs_s, c_s, acc_s = (r.at[sub] for r in
                                                 (z_ref, s_ref, ls_ref, e_ref, rs_ref, c_ref, acc_ref))
        q = q_ref[q_rows, :] * (HEAD_DIM ** -0.5 * math.log2(math.e))
        qh = (jnp.where(head0, q, 0.0).astype(BF16), jnp.where(head0, 0.0, q).astype(BF16))

        def key_tile(m):
            return jnp.maximum(n_tiles - 1 - m, 0)

        def tile_rows(m):
            return pl.ds(pl.multiple_of(key_tile(m) * tk, tk), tk)

        def stage_a1(m, slot):
            kb = k_ref[tile_rows(m), :].astype(BF16)
            for h in range(2):
                z_s[h, slot] = _dot_nt(qh[h], kb)

        def stage_a2(slot, masked_tile=None, guard_tile=None):
            for h in range(2):
                z = z_s[h, slot]
                sp = jnp.maximum(z, 0.0) + jnp.log2(1.0 + jnp.exp2(-jnp.abs(z)))
                log_beta = z - sp
                if guard_tile is not None:
                    sp = jnp.where(guard_tile < n_tiles, sp, 0.0)
                    log_beta = jnp.where(guard_tile < n_tiles, log_beta, NEG_BIG)
                if masked_tile is not None:
                    t_pos = qi * tq + lax.broadcasted_iota(jnp.int32, (tq, tk), 0)
                    s_pos = key_tile(masked_tile) * tk + lax.broadcasted_iota(jnp.int32, (tq, tk), 1)
                    sp = jnp.where(s_pos < t_pos, sp, 0.0)
                    log_beta = jnp.where(s_pos < t_pos, log_beta, NEG_BIG)
                s_s[h, slot] = sp.astype(BF16)
                ls_s[h, slot] = log_beta

        def stage_b(slot):
            for h in range(2):
                both = _dot(s_s[h, slot], tail_mat)
                e_s[h, slot] = ls_s[h, slot] + both[:, :tk]
                rs_s[h, slot] = both[:, tk:]

        def stage_c(m, slot):
            vb = v_ref[tile_rows(m), :].astype(BF16)
            for h in range(2):
                c = c_s[h]
                w = jnp.exp2(e_s[h, slot] + c)
                acc_s[h] += _dot(w.astype(BF16), vb)
                c_s[h] = c + rs_s[h, slot]

        def two_steps(state, guarded=False):
            pair, _ = state
            i = 2 * pair + 3
            for s_new, s_old in ((1, 0), (0, 1)):
                stage_a1(i, s_new)
                stage_b(s_new)
                stage_c(i - 3, s_old)
                stage_a2(s_old, guard_tile=i - 1 if guarded else None)
                i = i + 1
            return pair + 1, jnp.max(c_s[...]) >= DEAD_LOG2_WEIGHT

        def straight():
            c_s[...] = jnp.zeros((2, tq, tk), F32)
            acc_s[...] = jnp.zeros((2, tq, LANES), F32)
            stage_a1(0, 0)
            stage_a2(0, masked_tile=0)
            stage_a1(1, 1)
            stage_b(0)
            stage_a2(1, masked_tile=1)
            stage_a1(2, 0)
            return two_steps(two_steps((0, None), guarded=True), guarded=True)

        def rest(state):
            lax.while_loop(lambda st: (st[0] < n_tiles // 2) & st[1], two_steps,
                           (jnp.int32(state[0]), state[1]))
            o_ref[q_rows, :] = jnp.where(head0, acc_s[0], acc_s[1]).astype(o_ref.dtype)

        return straight, rest

    walkers = [walker(sub) for sub in range(SB_TILES_PER_STEP)]
    states = [straight() for straight, _ in walkers]
    for (_, rest), state in zip(walkers, states):
        rest(state)


def _sb_attention(z, *, batch, seq, tq):
    n = z.shape[0]
    rows = SB_TILES_PER_STEP * tq
    nq = seq // rows
    pairs = SB_WIDTH // LANES
    sub = SB_TILES_PER_STEP
    return pl.pallas_call(
        functools.partial(_sb_kernel, tq=tq),
        grid=(batch, pairs, nq),
        in_specs=[
            pl.BlockSpec((rows, LANES), lambda b, p, i: (b * nq + i, p)),
            pl.BlockSpec((seq, LANES), lambda b, p, i: (b, pairs + p)),
            pl.BlockSpec((seq, LANES), lambda b, p, i: (b, 2 * pairs + p)),
        ],
        out_specs=pl.BlockSpec((rows, LANES), lambda b, p, i: (b * nq + i, p)),
        out_shape=jax.ShapeDtypeStruct((n, SB_WIDTH), BF16),
        scratch_shapes=[
            pltpu.VMEM((sub, 2, 2, tq, SB_KEY_TILE), F32),
            pltpu.VMEM((sub, 2, 2, tq, SB_KEY_TILE), BF16),
            pltpu.VMEM((sub, 2, 2, tq, SB_KEY_TILE), F32),
            pltpu.VMEM((sub, 2, 2, tq, SB_KEY_TILE), F32),
            pltpu.VMEM((sub, 2, 2, tq, SB_KEY_TILE), F32),
            pltpu.VMEM((sub, 2, tq, SB_KEY_TILE), F32),
            pltpu.VMEM((sub, 2, tq, LANES), F32),
        ],
        compiler_params=_params("parallel", "parallel", "arbitrary"),
        name="stick_breaking",
    )(z, z, z)


def _rwkv_kernel(r_ref, k_ref, v_ref, lo_ref, mur_ref, muk_ref, muv_ref, mulo_ref,
                 wup_ref, w0_ref, aup_ref, a0_ref, gup_ref, kk_ref, ka_ref, rk_ref,
                 lnw_ref, lnb_ref, o_ref, s_ref, pr_ref, pk_ref, pv_ref, plo_ref, *, tc):
    C = RW_CHUNK
    C2 = 2 * C
    nchunk = tc // C
    npair = RW_PAIRS_PER_STEP
    width = npair * LANES
    t = pl.program_id(2)

    @pl.when(t == 0)
    def _():
        s_ref[...] = jnp.zeros_like(s_ref)
        pr_ref[...] = jnp.zeros_like(pr_ref)
        pk_ref[...] = jnp.zeros_like(pk_ref)
        pv_ref[...] = jnp.zeros_like(pv_ref)
        plo_ref[...] = jnp.zeros_like(plo_ref)

    def shifted(z_ref, prev_ref, mu_ref):
        z = z_ref[...]
        row = lax.broadcasted_iota(jnp.int32, z.shape, 0)
        prev = jnp.where(row == 0, prev_ref[0:1, :], pltpu.roll(z, 1, axis=0))
        prev_ref[0:1, :] = z[tc - 1:tc, :]
        return z + (prev - z) * mu_ref[...]

    r = shifted(r_ref, pr_ref, mur_ref)
    k = shifted(k_ref, pk_ref, muk_ref)
    v = shifted(v_ref, pv_ref, muv_ref)
    lo = shifted(lo_ref, plo_ref, mulo_ref)
    xw = lo[:, :DECAY_LORA]
    xa = lo[:, DECAY_LORA:DECAY_LORA + ICL_LORA]
    xg = lo[:, DECAY_LORA + ICL_LORA:]

    wl = w0_ref[...] + _dot(jnp.tanh(xw).astype(BF16), wup_ref[...])
    w_log = -(jnp.maximum(-wl, 0.0) + jnp.log1p(jnp.exp(-jnp.abs(wl)))) - 0.5
    lw = -jnp.exp(w_log)
    a = jax.nn.sigmoid(a0_ref[...] + _dot(xa.astype(BF16), aup_ref[...]))
    g = _dot(jax.nn.sigmoid(xg).astype(BF16), gup_ref[...])

    li = lax.broadcasted_iota(jnp.int32, (width, width), 0)
    lj = lax.broadcasted_iota(jnp.int32, (width, width), 1)
    head_ones = jnp.where(li // HEAD_DIM == lj // HEAD_DIM, 1.0, 0.0).astype(BF16)

    def head_sum(x):
        return _dot(x.astype(BF16), head_ones)

    kk = k * kk_ref[...]
    kk = kk / jnp.maximum(jnp.sqrt(head_sum(kk * kk)), 1e-12)
    kmod = k * (1.0 + (a - 1.0) * ka_ref[...])
    bonus = head_sum(r * kmod * rk_ref[...]) * v
    kka = kk * a

    lane = lax.broadcasted_iota(jnp.int32, (C, LANES), 1)
    head0 = lane < HEAD_DIM

    def stack(x):
        return jnp.concatenate([jnp.where(head0, x, 0.0), jnp.where(head0, 0.0, x)], axis=0)

    def unstack(x):
        return x[:C] + x[C:]

    ci = lax.broadcasted_iota(jnp.int32, (C, C2), 0)
    cj = lax.broadcasted_iota(jnp.int32, (C, C2), 1) & (C - 1)
    cum_mat = jnp.where(cj <= ci, 1.0, 0.0).astype(BF16)
    lw_hi, lw_lo = _split_bf16(lw)
    cum = jnp.concatenate(
        [_dot(cum_mat, jnp.concatenate([lw_hi[c * C:(c + 1) * C], lw_lo[c * C:(c + 1) * C]], axis=0))
         for c in range(nchunk)], axis=0)
    e_cum = jnp.exp(cum)
    e_inv = jnp.exp(-cum)
    rt_all = r * e_cum
    bt_all = kk * jnp.exp(cum - lw)
    at_all = -kka * e_inv
    kt_all = kmod * e_inv

    gi = lax.broadcasted_iota(jnp.int32, (2 * C2, 2 * C2), 0)
    gj = lax.broadcasted_iota(jnp.int32, (2 * C2, 2 * C2), 1)
    ti, tj = gi & (C - 1), gj & (C - 1)
    gram_mask = tj < ti + jnp.where(gi < C2, 0, 1)
    si = lax.broadcasted_iota(jnp.int32, (C2, C2), 0)
    sj = lax.broadcasted_iota(jnp.int32, (C2, C2), 1)
    eye = si == sj
    s_mask = (si // C) == (sj // C)

    zeros_c = jnp.zeros((C, LANES), F32)

    probs = [(c, p) for c in range(nchunk) for p in range(npair)]
    chunks = range(len(probs))
    sl = [(slice(c * C, (c + 1) * C), slice(p * LANES, (p + 1) * LANES)) for c, p in probs]
    rt = [rt_all[s] for s in sl]
    bt = [bt_all[s] for s in sl]
    at = [at_all[s] for s in sl]
    kt = [kt_all[s] for s in sl]
    vc = [v[s] for s in sl]
    p_last = [e_cum[(c + 1) * C - 1:(c + 1) * C, p * LANES:(p + 1) * LANES] for c, p in probs]

    gram = []
    for c in chunks:
        lhs = jnp.concatenate([stack(bt[c]), stack(rt[c])], axis=0).astype(BF16)
        rhs = jnp.concatenate([stack(at[c]), stack(kt[c])], axis=0).astype(BF16)
        gram.append(jnp.where(gram_mask, _dot_nt(lhs, rhs), 0.0))
    n_ab = [gm[:C2, :C2] for gm in gram]
    a_rows = [gm[C2:].astype(BF16) for gm in gram]
    sv = [stack(v_c).astype(BF16) for v_c in vc]
    x0 = [unstack(_dot(gram[c][:C2, C2:].astype(BF16), sv[c])) for c in chunks]

    tinv = [jnp.where(eye, 1.0, n) for n in n_ab]
    pw = [_dot(n.astype(BF16), n.astype(BF16)) for n in n_ab]
    for lvl in range(2, int(math.log2(C)) + 1):
        last = lvl == int(math.log2(C))
        rhs = [tinv[c].astype(BF16) if last else
               jnp.concatenate([tinv[c], pw[c]], axis=1).astype(BF16) for c in chunks]
        prod = [_dot(pw[c].astype(BF16), rhs[c]) for c in chunks]
        tinv = [tinv[c] + prod[c][:, :C2] for c in chunks]
        if not last:
            pw = [prod[c][:, C2:] for c in chunks]

    tb = [_dot(tinv[c].astype(BF16),
               jnp.concatenate([stack(bt[c]), stack(x0[c])], axis=1).astype(BF16)) for c in chunks]
    w_c = [unstack(m[:, :LANES]) for m in tb]
    u0 = [unstack(m[:, LANES:]) for m in tb]
    zeros_s = jnp.zeros((C2, LANES), BF16)
    ab = [_dot(a_rows[c], jnp.concatenate(
              [jnp.concatenate([stack(w_c[c]), stack(u0[c])], axis=1).astype(BF16),
               jnp.concatenate([zeros_s, sv[c]], axis=1)], axis=0)) for c in chunks]
    r_eff = [(rt[c] + unstack(ab[c][:, :LANES])).astype(BF16) for c in chunks]
    y0 = [unstack(ab[c][:, LANES:]) for c in chunks]
    tn = [_dot_tn(jnp.concatenate([jnp.concatenate([w_c[c], u0[c]], axis=1),
                                   jnp.concatenate([zeros_c, vc[c]], axis=1)], axis=0).astype(BF16),
                  jnp.concatenate([at[c], kt[c]], axis=0).astype(BF16)) for c in chunks]
    a_c = [(jnp.where(s_mask, m[:C2], 0.0) * p_last[c]).astype(BF16) for c, m in enumerate(tn)]
    b_c = [jnp.where(s_mask, m[C2:], 0.0) * p_last[c] for c, m in enumerate(tn)]

    s = [s_ref[p] for p in range(npair)]
    ys = []
    for i, (c, p) in enumerate(probs):
        sb = s[p].astype(BF16)
        ys.append(y0[i] + _dot_nt(r_eff[i], sb))
        s[p] = s[p] * p_last[i] + _dot(sb, a_c[i]) + b_c[i]
    for p in range(npair):
        s_ref[p] = s[p]

    y = jnp.concatenate([jnp.concatenate(ys[c * npair:(c + 1) * npair], axis=1)
                         for c in range(nchunk)], axis=0)
    mean = head_sum(y) * (1.0 / HEAD_DIM)
    yc = y - mean
    var = head_sum(yc * yc) * (1.0 / HEAD_DIM)
    yn = yc * lax.rsqrt(var + LNX_EPS) * lnw_ref[...] + lnb_ref[...]
    o_ref[...] = ((yn + bonus) * g).astype(o_ref.dtype)


def _rwkv(z, mu, w_up, w0, a_up, a0, g_up, k_k, k_a, r_k, lnx_w, lnx_b, *, batch, seq, tc):
    n = z.shape[0]
    nt = seq // tc
    width = RW_PAIRS_PER_STEP * LANES
    groups = RW_WIDTH // width
    col0 = 3 * SB_WIDTH // width
    lo_blk = (3 * SB_WIDTH + 3 * RW_WIDTH) // LORA_PAD
    vec = lambda off: pl.BlockSpec((1, width), lambda b, p, t: (0, off + p))
    return pl.pallas_call(
        functools.partial(_rwkv_kernel, tc=tc),
        grid=(batch, groups, nt),
        in_specs=[
            pl.BlockSpec((tc, width), lambda b, p, t: (b * nt + t, col0 + p)),
            pl.BlockSpec((tc, width), lambda b, p, t: (b * nt + t, col0 + groups + p)),
            pl.BlockSpec((tc, width), lambda b, p, t: (b * nt + t, col0 + 2 * groups + p)),
            pl.BlockSpec((tc, LORA_PAD), lambda b, p, t: (b * nt + t, lo_blk)),
            vec(0), vec(groups), vec(2 * groups),
            pl.BlockSpec((1, LORA_PAD), lambda b, p, t: (0, 3 * RW_WIDTH // LORA_PAD)),
            pl.BlockSpec((DECAY_LORA, width), lambda b, p, t: (0, p)),
            vec(0),
            pl.BlockSpec((ICL_LORA, width), lambda b, p, t: (0, p)),
            vec(0),
            pl.BlockSpec((LORA_PAD - DECAY_LORA - ICL_LORA, width), lambda b, p, t: (0, p)),
            vec(0), vec(0), vec(0), vec(0), vec(0),
        ],
        out_specs=pl.BlockSpec((tc, width), lambda b, p, t: (b * nt + t, p)),
        out_shape=jax.ShapeDtypeStruct((n, RW_WIDTH), BF16),
        scratch_shapes=[
            pltpu.VMEM((RW_PAIRS_PER_STEP, LANES, LANES), F32),
            pltpu.VMEM((8, width), F32), pltpu.VMEM((8, width), F32), pltpu.VMEM((8, width), F32),
            pltpu.VMEM((8, LORA_PAD), F32),
        ],
        compiler_params=_params("parallel", "parallel", "arbitrary"),
        name="rwkv7",
    )(z, z, z, z, mu, mu, mu, mu, w_up, w0, a_up, a0, g_up, k_k, k_a, r_k, lnx_w, lnx_b)


def _sgu_kernel(x_ref, u_ref, v_ref, lg_ref, lb_ref, ws_ref, bs_ref, wo_ref, o_ref, y_ref, *, tm):
    v = v_ref[...]
    mean = jnp.mean(v, axis=-1, keepdims=True)
    vc = v - mean
    var = jnp.mean(vc * vc, axis=-1, keepdims=True)
    vn = (vc * lax.rsqrt(var + LN_EPS) * lg_ref[...] + lb_ref[...]).astype(BF16)

    ti = lax.broadcasted_iota(jnp.int32, (SGU_BLOCK, SGU_BLOCK), 0) // STREAM_CHUNK
    si = lax.broadcasted_iota(jnp.int32, (SGU_BLOCK, SGU_BLOCK), 1) // STREAM_CHUNK
    causal = si <= ti
    gw = v.shape[1] // SGU_GROUPS
    for g in range(SGU_GROUPS):
        ws = jnp.where(causal, ws_ref[g], 0.0).astype(BF16)
        cols = slice(g * gw, (g + 1) * gw)
        for blk in range(tm // SGU_BLOCK):
            rows = slice(blk * SGU_BLOCK, (blk + 1) * SGU_BLOCK)
            mixed = _dot(ws, vn[rows, cols]) + bs_ref[:, g:g + 1]
            y_ref[rows, cols] = (u_ref[rows, cols] * mixed).astype(BF16)
    o_ref[...] = x_ref[...] + _dot(y_ref[...], wo_ref[...])


def _sgu(x, zc, ln_g, ln_b, w_s, b_s_t, w_out, *, tm):
    n, d = x.shape
    return pl.pallas_call(
        functools.partial(_sgu_kernel, tm=tm),
        grid=(n // tm,),
        in_specs=[
            pl.BlockSpec((tm, d), lambda i: (i, 0)),
            pl.BlockSpec((tm, d), lambda i: (i, 0)),
            pl.BlockSpec((tm, d), lambda i: (i, 1)),
            pl.BlockSpec((1, d), lambda i: (0, 0)),
            pl.BlockSpec((1, d), lambda i: (0, 0)),
            pl.BlockSpec((SGU_GROUPS, SGU_BLOCK, SGU_BLOCK), lambda i: (0, 0, 0)),
            pl.BlockSpec((SGU_BLOCK, SGU_GROUPS), lambda i: (0, 0)),
            pl.BlockSpec((d, d), lambda i: (0, 0)),
        ],
        out_specs=pl.BlockSpec((tm, d), lambda i: (i, 0)),
        out_shape=jax.ShapeDtypeStruct((n, d), F32),
        scratch_shapes=[pltpu.VMEM((tm, d), BF16)],
        compiler_params=_params("parallel"),
        name="sgu",
    )(x, zc, zc, ln_g, ln_b, w_s, b_s_t, w_out)


def _pad_cols(w, width):
    return jnp.pad(w, ((0, 0), (0, width - w.shape[1])))


def _pad_rows(w, height):
    return jnp.pad(w, ((0, height - w.shape[0]), (0, 0)))


def kernel(x, norm_mix, norm_ffn, norm_final, even_w_in, even_mu, even_w_up, even_w0, even_a_up, even_a0, even_g_up, even_k_k, even_k_a, even_r_k, even_lnx_w, even_lnx_b, even_w_out, odd_w_in, odd_ln_g, odd_ln_b, odd_w_s, odd_b_s, odd_w_out, ffn_w1, ffn_w2):
    batch, seq, d = x.shape
    depth = norm_mix.shape[0]
    assert (batch * seq) % ROW_TILE == 0 and seq % (SB_TILES_PER_STEP * SB_QUERY_TILE) == 0 and seq % RW_TIME_TILE == 0
    assert even_w_in.shape[2] == 3 * SB_WIDTH + 3 * RW_WIDTH + even_g_up.shape[1] + DECAY_LORA + ICL_LORA
    xf = x.reshape(batch * seq, d)
    row = lambda a: a.reshape(1, -1)

    rw_cols = 3 * RW_WIDTH + LORA_PAD
    in_cols = 3 * SB_WIDTH + rw_cols + LANES

    for layer in range(depth):
        i = layer // 2
        if layer % 2 == 0:
            w_in = _pad_cols(even_w_in[i], in_cols).astype(BF16)
            z = _norm_matmul(xf, row(norm_mix[layer]), w_in, tm=ROW_TILE, tn=in_cols // IN_PROJ_COL_STEPS)
            o_sb = _sb_attention(z, batch=batch, seq=seq, tq=SB_QUERY_TILE)
            o_rw = _rwkv(z, _pad_cols(row(even_mu[i]), rw_cols),
                         even_w_up[i].astype(BF16), row(even_w0[i]),
                         even_a_up[i].astype(BF16), row(even_a0[i]),
                         _pad_rows(even_g_up[i], LORA_PAD - DECAY_LORA - ICL_LORA).astype(BF16),
                         row(even_k_k[i]), row(even_k_a[i]),
                         row(even_r_k[i]), row(even_lnx_w[i]), row(even_lnx_b[i]),
                         batch=batch, seq=seq, tc=RW_TIME_TILE)
            xf = _proj2(xf, o_sb, o_rw, even_w_out[i].astype(BF16), tm=ROW_TILE)
        else:
            zc = _norm_matmul(xf, row(norm_mix[layer]), odd_w_in[i].astype(BF16),
                              tm=ROW_TILE, tn=GELU_COL_TILE, gelu=True)
            xf = _sgu(xf, zc, row(odd_ln_g[i]), row(odd_ln_b[i]), odd_w_s[i],
                      odd_b_s[i].T, odd_w_out[i].astype(BF16), tm=SGU_ROW_TILE)
        xf = _ffn(xf, row(norm_ffn[layer]), ffn_w1[layer].astype(BF16), ffn_w2[layer].astype(BF16),
                  row(norm_final), tm=ROW_TILE, th=FFN_HIDDEN_TILE, final_norm=(layer == depth - 1))
    return xf.reshape(batch, seq, d)
```

```python
import functools
import math

import jax
import jax.numpy as jnp
from jax import lax
from jax.experimental import pallas as pl
from jax.experimental.pallas import tpu as pltpu

F32 = jnp.float32
BF16 = jnp.bfloat16

LANES = 128
HEAD_DIM = 64
RMS_EPS = 1e-6
LN_EPS = 1e-5
LNX_EPS = 64e-5
SB_WIDTH = 1024
RW_WIDTH = 1024
DECAY_LORA = 64
ICL_LORA = 64
LORA_PAD = 384
SGU_GROUPS = 8
SGU_BLOCK = 128
STREAM_CHUNK = 64
RW_CHUNK = 64
RW_PAIRS_PER_STEP = 2

VMEM_LIMIT = 56 * 1024 * 1024
ROW_TILE = 512
FFN_HIDDEN_TILE = 2048
IN_PROJ_COL_STEPS = 2
GELU_COL_TILE = 2048
SGU_ROW_TILE = 256
SB_QUERY_TILE = 256
SB_TILES_PER_STEP = 8
RW_TIME_TILE = 512


def _params(*sem):
    return pltpu.CompilerParams(dimension_semantics=sem, vmem_limit_bytes=VMEM_LIMIT)


def _rms(x, g):
    return x * lax.rsqrt(jnp.mean(x * x, axis=-1, keepdims=True) + RMS_EPS) * g


def _dot(a, b):
    return jnp.dot(a, b, preferred_element_type=F32)


def _dot_nt(a, b):
    return lax.dot_general(a, b, (((1,), (1,)), ((), ())), preferred_element_type=F32)


def _dot_tn(a, b):
    return lax.dot_general(a, b, (((0,), (0,)), ((), ())), preferred_element_type=F32)


def _split_bf16(x):
    hi = x.astype(BF16)
    lo = (x - hi.astype(F32)).astype(BF16)
    return hi, lo


def _norm_matmul_kernel(x_ref, g_ref, w_ref, o_ref, h_ref, *, gelu):
    @pl.when(pl.program_id(1) == 0)
    def _():
        h_ref[...] = _rms(x_ref[...], g_ref[...]).astype(BF16)

    y = _dot(h_ref[...], w_ref[...])
    if gelu:
        y = 0.5 * y * (1.0 + lax.erf(y * math.sqrt(0.5)))
    o_ref[...] = y.astype(o_ref.dtype)


def _norm_matmul(x, g, w, *, tm, tn, gelu=False, out_dtype=F32):
    n, d = x.shape
    nout = w.shape[1]
    return pl.pallas_call(
        functools.partial(_norm_matmul_kernel, gelu=gelu),
        grid=(n // tm, nout // tn),
        in_specs=[
            pl.BlockSpec((tm, d), lambda i, j: (i, 0)),
            pl.BlockSpec((1, d), lambda i, j: (0, 0)),
            pl.BlockSpec((d, tn), lambda i, j: (0, j)),
        ],
        out_specs=pl.BlockSpec((tm, tn), lambda i, j: (i, j)),
        out_shape=jax.ShapeDtypeStruct((n, nout), out_dtype),
        scratch_shapes=[pltpu.VMEM((tm, d), BF16)],
        compiler_params=_params("parallel", "arbitrary"),
        name="norm_matmul_gelu" if gelu else "norm_matmul",
    )(x, g, w)


def _ffn_kernel(x_ref, g_ref, w1_ref, w2_ref, gf_ref, o_ref, h_ref, *, final_norm):
    j = pl.program_id(1)

    @pl.when(j == 0)
    def _():
        x = x_ref[...]
        h_ref[...] = _rms(x, g_ref[...]).astype(BF16)
        o_ref[...] = x

    a = jnp.square(jnp.maximum(_dot(h_ref[...], w1_ref[...]), 0.0)).astype(BF16)
    o_ref[...] += _dot(a, w2_ref[...])

    if final_norm:
        @pl.when(j == pl.num_programs(1) - 1)
        def _():
            o_ref[...] = _rms(o_ref[...], gf_ref[...])


def _ffn(x, g, w1, w2, gf, *, tm, th, final_norm):
    n, d = x.shape
    hid = w1.shape[1]
    return pl.pallas_call(
        functools.partial(_ffn_kernel, final_norm=final_norm),
        grid=(n // tm, hid // th),
        in_specs=[
            pl.BlockSpec((tm, d), lambda i, j: (i, 0)),
            pl.BlockSpec((1, d), lambda i, j: (0, 0)),
            pl.BlockSpec((d, th), lambda i, j: (0, j)),
            pl.BlockSpec((th, d), lambda i, j: (j, 0)),
            pl.BlockSpec((1, d), lambda i, j: (0, 0)),
        ],
        out_specs=pl.BlockSpec((tm, d), lambda i, j: (i, 0)),
        out_shape=jax.ShapeDtypeStruct((n, d), F32),
        scratch_shapes=[pltpu.VMEM((tm, d), BF16)],
        compiler_params=_params("parallel", "arbitrary"),
        name="ffn_final" if final_norm else "ffn",
    )(x, g, w1, w2, gf)


def _proj2_kernel(x_ref, a_ref, b_ref, w_ref, o_ref):
    ka = a_ref.shape[1]
    o_ref[...] = (x_ref[...] + _dot(a_ref[...], w_ref[:ka, :]) + _dot(b_ref[...], w_ref[ka:, :]))


def _proj2(x, a, b, w, *, tm):
    n, d = x.shape
    ka, kb = a.shape[1], b.shape[1]
    return pl.pallas_call(
        _proj2_kernel,
        grid=(n // tm,),
        in_specs=[
            pl.BlockSpec((tm, d), lambda i: (i, 0)),
            pl.BlockSpec((tm, ka), lambda i: (i, 0)),
            pl.BlockSpec((tm, kb), lambda i: (i, 0)),
            pl.BlockSpec((ka + kb, d), lambda i: (0, 0)),
        ],
        out_specs=pl.BlockSpec((tm, d), lambda i: (i, 0)),
        out_shape=jax.ShapeDtypeStruct((n, d), F32),
        compiler_params=_params("parallel"),
        name="out_proj",
    )(x, a, b, w)


NEG_BIG = -1e30
SB_KEY_TILE = 128
DEAD_LOG2_WEIGHT = -150.5


def _sb_kernel(q_ref, k_ref, v_ref, o_ref, z_ref, s_ref, ls_ref, e_ref, rs_ref, c_ref, acc_ref, *, tq):
    tk = SB_KEY_TILE
    per_q = tq // tk
    assert per_q == 2
    lane = lax.broadcasted_iota(jnp.int32, (tq, LANES), 1)
    head0 = lane < HEAD_DIM

    jj = lax.broadcasted_iota(jnp.int32, (tk, 2 * tk), 0)
    ss = lax.broadcasted_iota(jnp.int32, (tk, 2 * tk), 1)
    tail_mat = jnp.where((ss >= tk) | (jj > ss), -1.0, 0.0).astype(BF16)

    def walker(sub):
        qi = pl.program_id(2) * SB_TILES_PER_STEP + sub
        n_tiles = (qi + 1) * per_q
        q_rows = slice(sub * tq, (sub + 1) * tq)
        z_s, s_s, ls_s, e_s, rs_s, c_s, acc_s = (r.at[sub] for r in
                                                 (z_ref, s_ref, ls_ref, e_ref, rs_ref, c_ref, acc_ref))
        q = q_ref[q_rows, :] * (HEAD_DIM ** -0.5 * math.log2(math.e))
        qh = (jnp.where(head0, q, 0.0).astype(BF16), jnp.where(head0, 0.0, q).astype(BF16))

        def key_tile(m):
            return jnp.maximum(n_tiles - 1 - m, 0)

        def tile_rows(m):
            return pl.ds(pl.multiple_of(key_tile(m) * tk, tk), tk)

        def stage_a1(m, slot):
            kb = k_ref[tile_rows(m), :].astype(BF16)
            for h in range(2):
                z_s[h, slot] = _dot_nt(qh[h], kb)

        def stage_a2(slot, masked_tile=None, guard_tile=None):
            for h in range(2):
                z = z_s[h, slot]
                sp = jnp.maximum(z, 0.0) + jnp.log2(1.0 + jnp.exp2(-jnp.abs(z)))
                log_beta = z - sp
                if guard_tile is not None:
                    sp = jnp.where(guard_tile < n_tiles, sp, 0.0)
                    log_beta = jnp.where(guard_tile < n_tiles, log_beta, NEG_BIG)
                if masked_tile is not None:
                    t_pos = qi * tq + lax.broadcasted_iota(jnp.int32, (tq, tk), 0)
                    s_pos = key_tile(masked_tile) * tk + lax.broadcasted_iota(jnp.int32, (tq, tk), 1)
                    sp = jnp.where(s_pos < t_pos, sp, 0.0)
                    log_beta = jnp.where(s_pos < t_pos, log_beta, NEG_BIG)
                s_s[h, slot] = sp.astype(BF16)
                ls_s[h, slot] = log_beta

        def stage_b(slot):
            for h in range(2):
                both = _dot(s_s[h, slot], tail_mat)
                e_s[h, slot] = ls_s[h, slot] + both[:, :tk]
                rs_s[h, slot] = both[:, tk:]

        def stage_c(m, slot):
            vb = v_ref[tile_rows(m), :].astype(BF16)
            for h in range(2):
                c = c_s[h]
                w = jnp.exp2(e_s[h, slot] + c)
                acc_s[h] += _dot(w.astype(BF16), vb)
                c_s[h] = c + rs_s[h, slot]

        def two_steps(state, guarded=False):
            pair, _ = state
            i = 2 * pair + 3
            for s_new, s_old in ((1, 0), (0, 1)):
                stage_a1(i, s_new)
                stage_b(s_new)
                stage_c(i - 3, s_old)
                stage_a2(s_old, guard_tile=i - 1 if guarded else None)
                i = i + 1
            return pair + 1, jnp.max(c_s[...]) >= DEAD_LOG2_WEIGHT

        def straight():
            c_s[...] = jnp.zeros((2, tq, tk), F32)
            acc_s[...] = jnp.zeros((2, tq, LANES), F32)
            stage_a1(0, 0)
            stage_a2(0, masked_tile=0)
            stage_a1(1, 1)
            stage_b(0)
            stage_a2(1, masked_tile=1)
            stage_a1(2, 0)
            return two_steps(two_steps((0, None), guarded=True), guarded=True)

        def rest(state):
            lax.while_loop(lambda st: (st[0] < n_tiles // 2) & st[1], two_steps,
                           (jnp.int32(state[0]), state[1]))
            o_ref[q_rows, :] = jnp.where(head0, acc_s[0], acc_s[1]).astype(o_ref.dtype)

        return straight, rest

    walkers = [walker(sub) for sub in range(SB_TILES_PER_STEP)]
    states = [straight() for straight, _ in walkers]
    for (_, rest), state in zip(walkers, states):
        rest(state)


def _sb_attention(z, *, batch, seq, tq):
    n = z.shape[0]
    rows = SB_TILES_PER_STEP * tq
    nq = seq // rows
    pairs = SB_WIDTH // LANES
    sub = SB_TILES_PER_STEP
    return pl.pallas_call(
        functools.partial(_sb_kernel, tq=tq),
        grid=(batch, pairs, nq),
        in_specs=[
            pl.BlockSpec((rows, LANES), lambda b, p, i: (b * nq + i, p)),
            pl.BlockSpec((seq, LANES), lambda b, p, i: (b, pairs + p)),
            pl.BlockSpec((seq, LANES), lambda b, p, i: (b, 2 * pairs + p)),
        ],
        out_specs=pl.BlockSpec((rows, LANES), lambda b, p, i: (b * nq + i, p)),
        out_shape=jax.ShapeDtypeStruct((n, SB_WIDTH), BF16),
        scratch_shapes=[
            pltpu.VMEM((sub, 2, 2, tq, SB_KEY_TILE), F32),
            pltpu.VMEM((sub, 2, 2, tq, SB_KEY_TILE), BF16),
            pltpu.VMEM((sub, 2, 2, tq, SB_KEY_TILE), F32),
            pltpu.VMEM((sub, 2, 2, tq, SB_KEY_TILE), F32),
            pltpu.VMEM((sub, 2, 2, tq, SB_KEY_TILE), F32),
            pltpu.VMEM((sub, 2, tq, SB_KEY_TILE), F32),
            pltpu.VMEM((sub, 2, tq, LANES), F32),
        ],
        compiler_params=_params("parallel", "parallel", "arbitrary"),
        name="stick_breaking",
    )(z, z, z)


def _rwkv_kernel(r_ref, k_ref, v_ref, lo_ref, mur_ref, muk_ref, muv_ref, mulo_ref,
                 wup_ref, w0_ref, aup_ref, a0_ref, gup_ref, kk_ref, ka_ref, rk_ref,
                 lnw_ref, lnb_ref, o_ref, s_ref, pr_ref, pk_ref, pv_ref, plo_ref, *, tc):
    C = RW_CHUNK
    C2 = 2 * C
    nchunk = tc // C
    npair = RW_PAIRS_PER_STEP
    width = npair * LANES
    t = pl.program_id(2)

    @pl.when(t == 0)
    def _():
        s_ref[...] = jnp.zeros_like(s_ref)
        pr_ref[...] = jnp.zeros_like(pr_ref)
        pk_ref[...] = jnp.zeros_like(pk_ref)
        pv_ref[...] = jnp.zeros_like(pv_ref)
        plo_ref[...] = jnp.zeros_like(plo_ref)

    def shifted(z_ref, prev_ref, mu_ref):
        z = z_ref[...]
        row = lax.broadcasted_iota(jnp.int32, z.shape, 0)
        prev = jnp.where(row == 0, prev_ref[0:1, :], pltpu.roll(z, 1, axis=0))
        prev_ref[0:1, :] = z[tc - 1:tc, :]
        return z + (prev - z) * mu_ref[...]

    r = shifted(r_ref, pr_ref, mur_ref)
    k = shifted(k_ref, pk_ref, muk_ref)
    v = shifted(v_ref, pv_ref, muv_ref)
    lo = shifted(lo_ref, plo_ref, mulo_ref)
    xw = lo[:, :DECAY_LORA]
    xa = lo[:, DECAY_LORA:DECAY_LORA + ICL_LORA]
    xg = lo[:, DECAY_LORA + ICL_LORA:]

    wl = w0_ref[...] + _dot(jnp.tanh(xw).astype(BF16), wup_ref[...])
    w_log = -(jnp.maximum(-wl, 0.0) + jnp.log1p(jnp.exp(-jnp.abs(wl)))) - 0.5
    lw = -jnp.exp(w_log)
    a = jax.nn.sigmoid(a0_ref[...] + _dot(xa.astype(BF16), aup_ref[...]))
    g = _dot(jax.nn.sigmoid(xg).astype(BF16), gup_ref[...])

    li = lax.broadcasted_iota(jnp.int32, (width, width), 0)
    lj = lax.broadcasted_iota(jnp.int32, (width, width), 1)
    head_ones = jnp.where(li // HEAD_DIM == lj // HEAD_DIM, 1.0, 0.0).astype(BF16)

    def head_sum(x):
        return _dot(x.astype(BF16), head_ones)

    kk = k * kk_ref[...]
    kk = kk / jnp.maximum(jnp.sqrt(head_sum(kk * kk)), 1e-12)
    kmod = k * (1.0 + (a - 1.0) * ka_ref[...])
    bonus = head_sum(r * kmod * rk_ref[...]) * v
    kka = kk * a

    lane = lax.broadcasted_iota(jnp.int32, (C, LANES), 1)
    head0 = lane < HEAD_DIM

    def stack(x):
        return jnp.concatenate([jnp.where(head0, x, 0.0), jnp.where(head0, 0.0, x)], axis=0)

    def unstack(x):
        return x[:C] + x[C:]

    ci = lax.broadcasted_iota(jnp.int32, (C, C2), 0)
    cj = lax.broadcasted_iota(jnp.int32, (C, C2), 1) & (C - 1)
    cum_mat = jnp.where(cj <= ci, 1.0, 0.0).astype(BF16)
    lw_hi, lw_lo = _split_bf16(lw)
    cum = jnp.concatenate(
        [_dot(cum_mat, jnp.concatenate([lw_hi[c * C:(c + 1) * C], lw_lo[c * C:(c + 1) * C]], axis=0))
         for c in range(nchunk)], axis=0)
    e_cum = jnp.exp(cum)
    e_inv = jnp.exp(-cum)
    rt_all = r * e_cum
    bt_all = kk * jnp.exp(cum - lw)
    at_all = -kka * e_inv
    kt_all = kmod * e_inv

    gi = lax.broadcasted_iota(jnp.int32, (2 * C2, 2 * C2), 0)
    gj = lax.broadcasted_iota(jnp.int32, (2 * C2, 2 * C2), 1)
    ti, tj = gi & (C - 1), gj & (C - 1)
    gram_mask = tj < ti + jnp.where(gi < C2, 0, 1)
    si = lax.broadcasted_iota(jnp.int32, (C2, C2), 0)
    sj = lax.broadcasted_iota(jnp.int32, (C2, C2), 1)
    eye = si == sj
    s_mask = (si // C) == (sj // C)

    zeros_c = jnp.zeros((C, LANES), F32)

    probs = [(c, p) for c in range(nchunk) for p in range(npair)]
    chunks = range(len(probs))
    sl = [(slice(c * C, (c + 1) * C), slice(p * LANES, (p + 1) * LANES)) for c, p in probs]
    rt = [rt_all[s] for s in sl]
    bt = [bt_all[s] for s in sl]
    at = [at_all[s] for s in sl]
    kt = [kt_all[s] for s in sl]
    vc = [v[s] for s in sl]
    p_last = [e_cum[(c + 1) * C - 1:(c + 1) * C, p * LANES:(p + 1) * LANES] for c, p in probs]

    gram = []
    for c in chunks:
        lhs = jnp.concatenate([stack(bt[c]), stack(rt[c])], axis=0).astype(BF16)
        rhs = jnp.concatenate([stack(at[c]), stack(kt[c])], axis=0).astype(BF16)
        gram.append(jnp.where(gram_mask, _dot_nt(lhs, rhs), 0.0))
    n_ab = [gm[:C2, :C2] for gm in gram]
    a_rows = [gm[C2:].astype(BF16) for gm in gram]
    sv = [stack(v_c).astype(BF16) for v_c in vc]
    x0 = [unstack(_dot(gram[c][:C2, C2:].astype(BF16), sv[c])) for c in chunks]

    tinv = [jnp.where(eye, 1.0, n) for n in n_ab]
    pw = [_dot(n.astype(BF16), n.astype(BF16)) for n in n_ab]
    for lvl in range(2, int(math.log2(C)) + 1):
        last = lvl == int(math.log2(C))
        rhs = [tinv[c].astype(BF16) if last else
               jnp.concatenate([tinv[c], pw[c]], axis=1).astype(BF16) for c in chunks]
        prod = [_dot(pw[c].astype(BF16), rhs[c]) for c in chunks]
        tinv = [tinv[c] + prod[c][:, :C2] for c in chunks]
        if not last:
            pw = [prod[c][:, C2:] for c in chunks]

    tb = [_dot(tinv[c].astype(BF16),
               jnp.concatenate([stack(bt[c]), stack(x0[c])], axis=1).astype(BF16)) for c in chunks]
    w_c = [unstack(m[:, :LANES]) for m in tb]
    u0 = [unstack(m[:, LANES:]) for m in tb]
    zeros_s = jnp.zeros((C2, LANES), BF16)
    ab = [_dot(a_rows[c], jnp.concatenate(
              [jnp.concatenate([stack(w_c[c]), stack(u0[c])], axis=1).astype(BF16),
               jnp.concatenate([zeros_s, sv[c]], axis=1)], axis=0)) for c in chunks]
    r_eff = [(rt[c] + unstack(ab[c][:, :LANES])).astype(BF16) for c in chunks]
    y0 = [unstack(ab[c][:, LANES:]) for c in chunks]
    tn = [_dot_tn(jnp.concatenate([jnp.concatenate([w_c[c], u0[c]], axis=1),
                                   jnp.concatenate([zeros_c, vc[c]], axis=1)], axis=0).astype(BF16),
                  jnp.concatenate([at[c], kt[c]], axis=0).astype(BF16)) for c in chunks]
    a_c = [(jnp.where(s_mask, m[:C2], 0.0) * p_last[c]).astype(BF16) for c, m in enumerate(tn)]
    b_c = [jnp.where(s_mask, m[C2:], 0.0) * p_last[c] for c, m in enumerate(tn)]

    s = [s_ref[p] for p in range(npair)]
    ys = []
    for i, (c, p) in enumerate(probs):
        sb = s[p].astype(BF16)
        ys.append(y0[i] + _dot_nt(r_eff[i], sb))
        s[p] = s[p] * p_last[i] + _dot(sb, a_c[i]) + b_c[i]
    for p in range(npair):
        s_ref[p] = s[p]

    y = jnp.concatenate([jnp.concatenate(ys[c * npair:(c + 1) * npair], axis=1)
                         for c in range(nchunk)], axis=0)
    mean = head_sum(y) * (1.0 / HEAD_DIM)
    yc = y - mean
    var = head_sum(yc * yc) * (1.0 / HEAD_DIM)
    yn = yc * lax.rsqrt(var + LNX_EPS) * lnw_ref[...] + lnb_ref[...]
    o_ref[...] = ((yn + bonus) * g).astype(o_ref.dtype)


def _rwkv(z, mu, w_up, w0, a_up, a0, g_up, k_k, k_a, r_k, lnx_w, lnx_b, *, batch, seq, tc):
    n = z.shape[0]
    nt = seq // tc
    width = RW_PAIRS_PER_STEP * LANES
    groups = RW_WIDTH // width
    col0 = 3 * SB_WIDTH // width
    lo_blk = (3 * SB_WIDTH + 3 * RW_WIDTH) // LORA_PAD
    vec = lambda off: pl.BlockSpec((1, width), lambda b, p, t: (0, off + p))
    return pl.pallas_call(
        functools.partial(_rwkv_kernel, tc=tc),
        grid=(batch, groups, nt),
        in_specs=[
            pl.BlockSpec((tc, width), lambda b, p, t: (b * nt + t, col0 + p)),
            pl.BlockSpec((tc, width), lambda b, p, t: (b * nt + t, col0 + groups + p)),
            pl.BlockSpec((tc, width), lambda b, p, t: (b * nt + t, col0 + 2 * groups + p)),
            pl.BlockSpec((tc, LORA_PAD), lambda b, p, t: (b * nt + t, lo_blk)),
            vec(0), vec(groups), vec(2 * groups),
            pl.BlockSpec((1, LORA_PAD), lambda b, p, t: (0, 3 * RW_WIDTH // LORA_PAD)),
            pl.BlockSpec((DECAY_LORA, width), lambda b, p, t: (0, p)),
            vec(0),
            pl.BlockSpec((ICL_LORA, width), lambda b, p, t: (0, p)),
            vec(0),
            pl.BlockSpec((LORA_PAD - DECAY_LORA - ICL_LORA, width), lambda b, p, t: (0, p)),
            vec(0), vec(0), vec(0), vec(0), vec(0),
        ],
        out_specs=pl.BlockSpec((tc, width), lambda b, p, t: (b * nt + t, p)),
        out_shape=jax.ShapeDtypeStruct((n, RW_WIDTH), BF16),
        scratch_shapes=[
            pltpu.VMEM((RW_PAIRS_PER_STEP, LANES, LANES), F32),
            pltpu.VMEM((8, width), F32), pltpu.VMEM((8, width), F32), pltpu.VMEM((8, width), F32),
            pltpu.VMEM((8, LORA_PAD), F32),
        ],
        compiler_params=_params("parallel", "parallel", "arbitrary"),
        name="rwkv7",
    )(z, z, z, z, mu, mu, mu, mu, w_up, w0, a_up, a0, g_up, k_k, k_a, r_k, lnx_w, lnx_b)


def _sgu_kernel(x_ref, u_ref, v_ref, lg_ref, lb_ref, ws_ref, bs_ref, wo_ref, o_ref, y_ref, *, tm):
    v = v_ref[...]
    mean = jnp.mean(v, axis=-1, keepdims=True)
    vc = v - mean
    var = jnp.mean(vc * vc, axis=-1, keepdims=True)
    vn = (vc * lax.rsqrt(var + LN_EPS) * lg_ref[...] + lb_ref[...]).astype(BF16)

    ti = lax.broadcasted_iota(jnp.int32, (SGU_BLOCK, SGU_BLOCK), 0) // STREAM_CHUNK
    si = lax.broadcasted_iota(jnp.int32, (SGU_BLOCK, SGU_BLOCK), 1) // STREAM_CHUNK
    causal = si <= ti
    gw = v.shape[1] // SGU_GROUPS
    for g in range(SGU_GROUPS):
        ws = jnp.where(causal, ws_ref[g], 0.0).astype(BF16)
        cols = slice(g * gw, (g + 1) * gw)
        for blk in range(tm // SGU_BLOCK):
            rows = slice(blk * SGU_BLOCK, (blk + 1) * SGU_BLOCK)
            mixed = _dot(ws, vn[rows, cols]) + bs_ref[:, g:g + 1]
            y_ref[rows, cols] = (u_ref[rows, cols] * mixed).astype(BF16)
    o_ref[...] = x_ref[...] + _dot(y_ref[...], wo_ref[...])


def _sgu(x, zc, ln_g, ln_b, w_s, b_s_t, w_out, *, tm):
    n, d = x.shape
    return pl.pallas_call(
        functools.partial(_sgu_kernel, tm=tm),
        grid=(n // tm,),
        in_specs=[
            pl.BlockSpec((tm, d), lambda i: (i, 0)),
            pl.BlockSpec((tm, d), lambda i: (i, 0)),
            pl.BlockSpec((tm, d), lambda i: (i, 1)),
            pl.BlockSpec((1, d), lambda i: (0, 0)),
            pl.BlockSpec((1, d), lambda i: (0, 0)),
            pl.BlockSpec((SGU_GROUPS, SGU_BLOCK, SGU_BLOCK), lambda i: (0, 0, 0)),
            pl.BlockSpec((SGU_BLOCK, SGU_GROUPS), lambda i: (0, 0)),
            pl.BlockSpec((d, d), lambda i: (0, 0)),
        ],
        out_specs=pl.BlockSpec((tm, d), lambda i: (i, 0)),
        out_shape=jax.ShapeDtypeStruct((n, d), F32),
        scratch_shapes=[pltpu.VMEM((tm, d), BF16)],
        compiler_params=_params("parallel"),
        name="sgu",
    )(x, zc, zc, ln_g, ln_b, w_s, b_s_t, w_out)


def _pad_cols(w, width):
    return jnp.pad(w, ((0, 0), (0, width - w.shape[1])))


def _pad_rows(w, height):
    return jnp.pad(w, ((0, height - w.shape[0]), (0, 0)))


def kernel(x, norm_mix, norm_ffn, norm_final, even_w_in, even_mu, even_w_up, even_w0, even_a_up, even_a0, even_g_up, even_k_k, even_k_a, even_r_k, even_lnx_w, even_lnx_b, even_w_out, odd_w_in, odd_ln_g, odd_ln_b, odd_w_s, odd_b_s, odd_w_out, ffn_w1, ffn_w2):
    batch, seq, d = x.shape
    depth = norm_mix.shape[0]
    assert (batch * seq) % ROW_TILE == 0 and seq % (SB_TILES_PER_STEP * SB_QUERY_TILE) == 0 and seq % RW_TIME_TILE == 0
    assert even_w_in.shape[2] == 3 * SB_WIDTH + 3 * RW_WIDTH + even_g_up.shape[1] + DECAY_LORA + ICL_LORA
    xf = x.reshape(batch * seq, d)
    row = lambda a: a.reshape(1, -1)

    rw_cols = 3 * RW_WIDTH + LORA_PAD
    in_cols = 3 * SB_WIDTH + rw_cols + LANES

    for layer in range(depth):
        i = layer // 2
        if layer % 2 == 0:
            w_in = _pad_cols(even_w_in[i], in_cols).astype(BF16)
            z = _norm_matmul(xf, row(norm_mix[layer]), w_in, tm=ROW_TILE, tn=in_cols // IN_PROJ_COL_STEPS)
            o_sb = _sb_attention(z, batch=batch, seq=seq, tq=SB_QUERY_TILE)
            o_rw = _rwkv(z, _pad_cols(row(even_mu[i]), rw_cols),
                         even_w_up[i].astype(BF16), row(even_w0[i]),
                         even_a_up[i].astype(BF16), row(even_a0[i]),
                         _pad_rows(even_g_up[i], LORA_PAD - DECAY_LORA - ICL_LORA).astype(BF16),
                         row(even_k_k[i]), row(even_k_a[i]),
                         row(even_r_k[i]), row(even_lnx_w[i]), row(even_lnx_b[i]),
                         batch=batch, seq=seq, tc=RW_TIME_TILE)
            xf = _proj2(xf, o_sb, o_rw, even_w_out[i].astype(BF16), tm=ROW_TILE)
        else:
            zc = _norm_matmul(xf, row(norm_mix[layer]), odd_w_in[i].astype(BF16),
                              tm=ROW_TILE, tn=GELU_COL_TILE, gelu=True)
            xf = _sgu(xf, zc, row(odd_ln_g[i]), row(odd_ln_b[i]), odd_w_s[i],
                      odd_b_s[i].T, odd_w_out[i].astype(BF16), tm=SGU_ROW_TILE)
        xf = _ffn(xf, row(norm_ffn[layer]), ffn_w1[layer].astype(BF16), ffn_w2[layer].astype(BF16),
                  row(norm_final), tm=ROW_TILE, th=FFN_HIDDEN_TILE, final_norm=(layer == depth - 1))
    return xf.reshape(batch, seq, d)
```
